```python
import math
import jax, jax.numpy as jnp
from jax import lax
import numpy as np

D_MODEL = 1024
BATCH = 8
SEQ = 2048
DEPTH = 2

HEAD_DIM = 64
A_Q_HEADS = 8
A_KV_HEADS = 2
A_GROUP = A_Q_HEADS // A_KV_HEADS
WINDOW = 128
B_HEADS = 8
MOBA_BLOCK = 256
MOBA_TOPK = 3
Q_CHUNK = 64
N_ALIBI = A_Q_HEADS + B_HEADS
D_FF = -(-(8 * D_MODEL) // (3 * 256)) * 256
N_MOD = 6
EPS = 1e-6

A_Q_W = A_Q_HEADS * HEAD_DIM
A_KV_W = A_KV_HEADS * HEAD_DIM
B_W = B_HEADS * HEAD_DIM
IN_WIDTHS = (A_Q_W, A_KV_W, A_KV_W, B_W, B_W, B_W, D_MODEL, D_MODEL)
IN_COLS = sum(IN_WIDTHS)

kernel_name = "hybrid_swa_sink_moba_gated_block"


def rms_norm(x, g):
    xf = x.astype(jnp.float32)
    y = xf * lax.rsqrt(jnp.mean(xf * xf, axis=-1, keepdims=True) + EPS)
    return (y * g.astype(jnp.float32)).astype(x.dtype)


def alibi_slopes():
    i = np.arange(1, N_ALIBI + 1, dtype=np.float32)
    s = jnp.asarray(2.0 ** (-8.0 * i / N_ALIBI), dtype=jnp.float32)
    return s[:A_Q_HEADS], s[A_Q_HEADS:]


def sliding_window_attention(q, k, v, sinks, slopes):
    B, S = q.shape[0], q.shape[1]
    nb = S // WINDOW
    scale = HEAD_DIM ** -0.5
    qb = q.reshape(B, nb, WINDOW, A_KV_HEADS, A_GROUP, HEAD_DIM)
    kb = k.reshape(B, nb, WINDOW, A_KV_HEADS, HEAD_DIM)
    vb = v.reshape(B, nb, WINDOW, A_KV_HEADS, HEAD_DIM)
    pad = ((0, 0), (1, 0), (0, 0), (0, 0), (0, 0))
    kc = jnp.concatenate([jnp.pad(kb, pad)[:, :-1], kb], axis=2)
    vc = jnp.concatenate([jnp.pad(vb, pad)[:, :-1], vb], axis=2)
    scores = jnp.einsum('bnqhgd,bnkhd->bhgnqk', qb, kc).astype(jnp.float32) * scale
    t = jnp.arange(WINDOW)[:, None] + WINDOW
    s = jnp.arange(2 * WINDOW)[None, :]
    dist = (t - s).astype(jnp.float32)
    valid = ((t - s) >= 0) & ((t - s) < WINDOW)
    block_ok = (jnp.arange(nb)[:, None, None] > 0) | (s[None] >= WINDOW)
    mask = valid[None] & block_ok
    m_h = slopes.reshape(1, A_KV_HEADS, A_GROUP, 1, 1, 1)
    scores = jnp.where(mask, scores - m_h * dist, -jnp.inf)
    sink = sinks.astype(jnp.float32).reshape(1, A_KV_HEADS, A_GROUP, 1, 1, 1)
    m = jnp.maximum(jnp.max(scores, axis=-1, keepdims=True), sink)
    p = jnp.exp(scores - m)
    denom = jnp.sum(p, axis=-1, keepdims=True) + jnp.exp(sink - m)
    out = jnp.einsum('bhgnqk,bnkhd->bnqhgd', (p / denom).astype(v.dtype), vc)
    return out.reshape(B, S, A_Q_W)


def moba_attention(q, k, v, slopes):
    B, S = q.shape[0], q.shape[1]
    Sp = -(-S // MOBA_BLOCK) * MOBA_BLOCK
    pad = ((0, 0), (0, Sp - S), (0, 0), (0, 0))
    q, k, v = jnp.pad(q, pad), jnp.pad(k, pad), jnp.pad(v, pad)
    nb = Sp // MOBA_BLOCK
    k_top = min(MOBA_TOPK, nb)
    nc = Sp // Q_CHUNK
    scale = HEAD_DIM ** -0.5
    kb = k.transpose(0, 2, 1, 3).reshape(B, B_HEADS, nb, MOBA_BLOCK, HEAD_DIM)
    vb = v.transpose(0, 2, 1, 3).reshape(B, B_HEADS, nb, MOBA_BLOCK, HEAD_DIM)
    k_mean = jnp.mean(kb.astype(jnp.float32), axis=3)
    q_chunks = jnp.moveaxis(q.transpose(0, 2, 1, 3).reshape(B, B_HEADS, nc, Q_CHUNK, HEAD_DIM), 2, 0)
    starts = jnp.arange(nc, dtype=jnp.int32) * Q_CHUNK
    bi = jnp.arange(B)[:, None, None]
    hi = jnp.arange(B_HEADS)[None, :, None]
    m_own = slopes.reshape(1, B_HEADS, 1, 1)
    m_sel = slopes.reshape(1, B_HEADS, 1, 1, 1)
    blk_pos = jnp.arange(MOBA_BLOCK, dtype=jnp.int32)

    def chunk_fn(args):
        q_c, start = args
        t = start + jnp.arange(Q_CHUNK, dtype=jnp.int32)
        own = start // MOBA_BLOCK
        gate = jnp.einsum('bhqd,bhnd->bhqn', q_c.astype(jnp.float32), k_mean)
        gate = jnp.where(jnp.arange(nb) < own, gate, -jnp.inf)
        _, idx = lax.top_k(gate, k_top)
        sel_ok = jnp.arange(k_top) < own
        k_own = lax.dynamic_index_in_dim(kb, own, axis=2, keepdims=False)
        v_own = lax.dynamic_index_in_dim(vb, own, axis=2, keepdims=False)
        d_o = t[:, None] - (own * MOBA_BLOCK + blk_pos)[None, :]
        s_o = jnp.einsum('bhqd,bhkd->bhqk', q_c, k_own).astype(jnp.float32) * scale
        s_o = jnp.where(d_o >= 0, s_o - m_own * d_o.astype(jnp.float32), -jnp.inf)
        flat = idx.reshape(B, B_HEADS, Q_CHUNK * k_top)
        k_sel = kb[bi, hi, flat].reshape(B, B_HEADS, Q_CHUNK, k_top, MOBA_BLOCK, HEAD_DIM)
        v_sel = vb[bi, hi, flat].reshape(B, B_HEADS, Q_CHUNK, k_top, MOBA_BLOCK, HEAD_DIM)
        s_s = jnp.einsum('bhqd,bhqrkd->bhqrk', q_c, k_sel).astype(jnp.float32) * scale
        d_s = t[None, None, :, None, None] - (idx[..., None] * MOBA_BLOCK + blk_pos)
        s_s = jnp.where(sel_ok[:, None], s_s - m_sel * d_s.astype(jnp.float32), -jnp.inf)
        m = jnp.maximum(jnp.max(s_o, axis=-1), jnp.max(s_s, axis=(-2, -1)))[..., None]
        p_o = jnp.exp(s_o - m)
        p_s = jnp.exp(s_s - m[..., None])
        denom = jnp.sum(p_o, axis=-1) + jnp.sum(p_s, axis=(-2, -1))
        num = (jnp.einsum('bhqk,bhkd->bhqd', p_o.astype(v.dtype), v_own).astype(jnp.float32)
               + jnp.einsum('bhqrk,bhqrkd->bhqd', p_s.astype(v.dtype), v_sel).astype(jnp.float32))
        return (num / denom[..., None]).astype(q_c.dtype)

    out = lax.map(chunk_fn, (q_chunks, starts))
    out = jnp.moveaxis(out, 0, 2).reshape(B, B_HEADS, Sp, HEAD_DIM)
    return out.transpose(0, 2, 1, 3).reshape(B, Sp, B_W)[:, :S]


def mixer(h, w_in, sinks, w_o_a, w_o_b, w_out):
    B, S, _ = h.shape
    proj = h @ w_in
    offsets = [int(o) for o in np.cumsum(IN_WIDTHS)[:-1]]
    a_q, a_k, a_v, b_q, b_k, b_v, g_a, g_b = jnp.split(proj, offsets, axis=-1)
    slopes_a, slopes_b = alibi_slopes()
    o_a = sliding_window_attention(
        a_q.reshape(B, S, A_Q_HEADS, HEAD_DIM), a_k.reshape(B, S, A_KV_HEADS, HEAD_DIM),
        a_v.reshape(B, S, A_KV_HEADS, HEAD_DIM), sinks, slopes_a) @ w_o_a
    o_b = moba_attention(
        b_q.reshape(B, S, B_HEADS, HEAD_DIM), b_k.reshape(B, S, B_HEADS, HEAD_DIM),
        b_v.reshape(B, S, B_HEADS, HEAD_DIM), slopes_b) @ w_o_b
    merged = jax.nn.sigmoid(g_a) * o_a + jax.nn.sigmoid(g_b) * o_b
    return merged @ w_out


def swiglu(h, w_gate_up, w_down):
    gate, up = jnp.split(h @ w_gate_up, 2, axis=-1)
    return (jax.nn.silu(gate) * up) @ w_down


def setup_inputs(seed: int = 0) -> dict:
    key = jax.random.key(seed)
    ks = jax.random.split(key, 16)
    nrm = lambda k, shape, s: jax.random.normal(k, shape, jnp.float32) * s
    L, D = DEPTH, D_MODEL
    return {
        "x": nrm(ks[0], (BATCH, SEQ, D), 1.0),
        "c": nrm(ks[1], (BATCH, D), 1.0),
        "ada_w": nrm(ks[2], (L, D, N_MOD * D), 0.5 * D ** -0.5),
        "ada_b": nrm(ks[3], (L, N_MOD * D), 0.02),
        "norm_pre_mix": 1.0 + nrm(ks[4], (L, D), 0.02),
        "norm_post_mix": 1.0 + nrm(ks[5], (L, D), 0.02),
        "w_in": nrm(ks[6], (L, D, IN_COLS), D ** -0.5),
        "attn_sinks": nrm(ks[7], (L, A_Q_HEADS), 0.5),
        "w_o_a": nrm(ks[8], (L, A_Q_W, D), A_Q_W ** -0.5),
        "w_o_b": nrm(ks[9], (L, B_W, D), B_W ** -0.5),
        "w_out": nrm(ks[10], (L, D, D), D ** -0.5),
        "norm_pre_ffn": 1.0 + nrm(ks[11], (L, D), 0.02),
        "norm_post_ffn": 1.0 + nrm(ks[12], (L, D), 0.02),
        "w_gate_up": nrm(ks[13], (L, D, 2 * D_FF), D ** -0.5),
        "w_down": nrm(ks[14], (L, D_FF, D), D_FF ** -0.5),
    }


def reference(x, c, ada_w, ada_b, norm_pre_mix, norm_post_mix, w_in, attn_sinks,
              w_o_a, w_o_b, w_out, norm_pre_ffn, norm_post_ffn, w_gate_up, w_down):
    cond = jax.nn.silu(c)
    for l in range(DEPTH):
        mod = (cond @ ada_w[l] + ada_b[l])[:, None, :]
        sh1, sc1, gt1, sh2, sc2, gt2 = jnp.split(mod, N_MOD, axis=-1)
        h = rms_norm(x, norm_pre_mix[l]) * (1.0 + sc1) + sh1
        y = mixer(h, w_in[l], attn_sinks[l], w_o_a[l], w_o_b[l], w_out[l])
        x = x + gt1 * rms_norm(y, norm_post_mix[l])
        h = rms_norm(x, norm_pre_ffn[l]) * (1.0 + sc2) + sh2
        y = swiglu(h, w_gate_up[l], w_down[l])
        x = x + gt2 * rms_norm(y, norm_post_ffn[l])
    return x
```

```python
import functools

import numpy as np
import jax
import jax.numpy as jnp
from jax import lax
from jax.experimental import pallas as pl
from jax.experimental.pallas import tpu as pltpu

HEAD_DIM = 64
A_Q_HEADS = 8
A_KV_HEADS = 2
A_GROUP = A_Q_HEADS // A_KV_HEADS
WINDOW = 128
B_HEADS = 8
MOBA_BLOCK = 256
MOBA_TOPK = 3
N_ALIBI = A_Q_HEADS + B_HEADS
N_MOD = 6
EPS = 1e-6

LANES = 128
HEADS_PER_LANE_BLOCK = LANES // HEAD_DIM
A_Q_W = A_Q_HEADS * HEAD_DIM
A_KV_W = A_KV_HEADS * HEAD_DIM
B_W = B_HEADS * HEAD_DIM
N_A_BLOCKS = A_Q_W // LANES
N_B_BLOCKS = B_W // LANES
VMEM_LIMIT_BYTES = 56 * 1024 * 1024

A_HEAD_ORDER = tuple(h for f in range(N_A_BLOCKS) for h in (f, A_GROUP + f))

F32 = jnp.float32
BF16 = jnp.bfloat16
NEG_INF = float("-inf")


def _dot(a, b):
    return jnp.dot(a, b, preferred_element_type=F32)


def _dot_nt(a, b):
    return lax.dot_general(a, b, (((1,), (1,)), ((), ())), preferred_element_type=F32)


def _rms(x, g):
    return x * lax.rsqrt(jnp.mean(x * x, axis=-1, keepdims=True) + EPS) * g


def _params(*semantics):
    return pltpu.CompilerParams(dimension_semantics=semantics, vmem_limit_bytes=VMEM_LIMIT_BYTES)


def _ada_kernel(c_ref, w_ref, b_ref, o_ref):
    c = c_ref[...]
    cond = c * jax.nn.sigmoid(c)
    w = w_ref[0]
    c_hi = cond.astype(BF16)
    c_lo = (cond - c_hi.astype(F32)).astype(BF16)
    w_hi = w.astype(BF16)
    w_lo = (w - w_hi.astype(F32)).astype(BF16)
    o_ref[0] = _dot(c_hi, w_hi) + _dot(c_hi, w_lo) + _dot(c_lo, w_hi) + b_ref[0]


def _ada(c, ada_w, ada_b):
    depth, d, n = ada_w.shape
    batch = c.shape[0]
    tn = 1024
    return pl.pallas_call(
        _ada_kernel,
        grid=(depth, n // tn),
        in_specs=[
            pl.BlockSpec((batch, d), lambda l, j: (0, 0)),
            pl.BlockSpec((1, d, tn), lambda l, j: (l, 0, j)),
            pl.BlockSpec((1, 1, tn), lambda l, j: (l, 0, j)),
        ],
        out_specs=pl.BlockSpec((1, batch, tn), lambda l, j: (l, 0, j)),
        out_shape=jax.ShapeDtypeStruct((depth, batch, n), F32),
        compiler_params=_params("parallel", "parallel"),
        name="ada_mod",
    )(c, ada_w, ada_b.reshape(depth, 1, n))


def _inproj_kernel(x_ref, mod_ref, g_ref, wstd_ref, wt_ref,
                   gg_ref, kb_ref, ka_ref, qta_ref, qtb_ref, vtb_ref, vta_ref, *, d_model):
    tm = x_ref.shape[0]
    x = x_ref[...]
    h = _rms(x, g_ref[...]) * (1.0 + mod_ref[0, 1:2, :]) + mod_ref[0, 0:1, :]
    hb = h.astype(BF16)

    n_gate = 2 * d_model
    for c0 in range(0, n_gate, 512):
        gg_ref[:, c0:c0 + 512] = _dot(hb, wstd_ref[:, c0:c0 + 512]).astype(BF16)
    kb = _dot(hb, wstd_ref[:, n_gate:n_gate + B_W])
    for f in range(N_B_BLOCKS):
        kb_ref[f] = kb[:, f * LANES:(f + 1) * LANES].astype(BF16)
    ka_ref[...] = _dot(hb, wstd_ref[:, n_gate + B_W:n_gate + B_W + A_KV_W]).astype(BF16)

    scale = HEAD_DIM ** -0.5
    qa = _dot_nt(wt_ref[0:A_Q_W, :], hb) * scale
    for f in range(N_A_BLOCKS):
        for t in range(tm // WINDOW):
            qta_ref[f, t] = qa[f * LANES:(f + 1) * LANES, t * WINDOW:(t + 1) * WINDOW].astype(BF16)
    qb = _dot_nt(wt_ref[A_Q_W:A_Q_W + B_W, :], hb) * scale
    for f in range(N_B_BLOCKS):
        for t in range(tm // MOBA_BLOCK):
            qtb_ref[f, t] = qb[f * LANES:(f + 1) * LANES, t * MOBA_BLOCK:(t + 1) * MOBA_BLOCK].astype(BF16)
    vb = _dot_nt(wt_ref[A_Q_W + B_W:A_Q_W + 2 * B_W, :], hb)
    for f in range(N_B_BLOCKS):
        for t in range(tm // MOBA_BLOCK):
            vtb_ref[f, t] = vb[f * LANES:(f + 1) * LANES, t * MOBA_BLOCK:(t + 1) * MOBA_BLOCK].astype(BF16)
    va = _dot_nt(wt_ref[A_Q_W + 2 * B_W:A_Q_W + 2 * B_W + A_KV_W, :], hb)
    for t in range(tm // WINDOW):
        vta_ref[t] = va[:, t * WINDOW:(t + 1) * WINDOW].astype(BF16)


def _inproj(x2, mod, g, w_std, w_t, seq):
    tokens, d = x2.shape
    tm = 512
    tiles_per_seq = seq // tm
    n_std = w_std.shape[1]
    n_t = w_t.shape[0]
    const = lambda i: (0, 0)
    out_shape = (
        jax.ShapeDtypeStruct((tokens, 2 * d), BF16),
        jax.ShapeDtypeStruct((N_B_BLOCKS, tokens, LANES), BF16),
        jax.ShapeDtypeStruct((tokens, LANES), BF16),
        jax.ShapeDtypeStruct((N_A_BLOCKS, tokens // WINDOW, LANES, WINDOW), BF16),
        jax.ShapeDtypeStruct((N_B_BLOCKS, tokens // MOBA_BLOCK, LANES, MOBA_BLOCK), BF16),
        jax.ShapeDtypeStruct((N_B_BLOCKS, tokens // MOBA_BLOCK, LANES, MOBA_BLOCK), BF16),
        jax.ShapeDtypeStruct((tokens // WINDOW, LANES, WINDOW), BF16),
    )
    out_specs = (
        pl.BlockSpec((tm, 2 * d), lambda i: (i, 0)),
        pl.BlockSpec((N_B_BLOCKS, tm, LANES), lambda i: (0, i, 0)),
        pl.BlockSpec((tm, LANES), lambda i: (i, 0)),
        pl.BlockSpec((N_A_BLOCKS, tm // WINDOW, LANES, WINDOW), lambda i: (0, i, 0, 0)),
        pl.BlockSpec((N_B_BLOCKS, tm // MOBA_BLOCK, LANES, MOBA_BLOCK), lambda i: (0, i, 0, 0)),
        pl.BlockSpec((N_B_BLOCKS, tm // MOBA_BLOCK, LANES, MOBA_BLOCK), lambda i: (0, i, 0, 0)),
        pl.BlockSpec((tm // WINDOW, LANES, WINDOW), lambda i: (i, 0, 0)),
    )
    return pl.pallas_call(
        functools.partial(_inproj_kernel, d_model=d),
        grid=(tokens // tm,),
        in_specs=[
            pl.BlockSpec((tm, d), lambda i: (i, 0)),
            pl.BlockSpec((1, N_MOD, d), lambda i: (i // tiles_per_seq, 0, 0)),
            pl.BlockSpec((1, d), const),
            pl.BlockSpec((d, n_std), const),
            pl.BlockSpec((n_t, d), const),
        ],
        out_specs=out_specs,
        out_shape=out_shape,
        compiler_params=_params("parallel"),
        name="in_proj",
    )(x2, mod, g, w_std, w_t)


def _swa_kernel(qt_ref, k_ref, vt_ref, slope_ref, sink_ref, o_ref):
    n_blocks = qt_ref.shape[1]
    slope = slope_ref[0]
    sink = sink_ref[0]
    lo_rows = lax.broadcasted_iota(jnp.int32, (LANES, WINDOW), 0) < HEAD_DIM

    def bias(n_keys, offset):
        r = lax.broadcasted_iota(jnp.int32, (n_keys, 2 * WINDOW), 0)
        col = lax.broadcasted_iota(jnp.int32, (n_keys, 2 * WINDOW), 1)
        qpos = jnp.where(col >= WINDOW, col - WINDOW, col)
        dist = offset + qpos - r
        valid = (dist >= 0) & (dist < WINDOW)
        return jnp.where(valid, -slope * dist.astype(F32), NEG_INF)

    def block(n, k_win, vt_win, b):
        qt = qt_ref[0, n]
        zero = jnp.zeros_like(qt)
        qt2 = jnp.concatenate([jnp.where(lo_rows, qt, zero), jnp.where(lo_rows, zero, qt)], axis=1)
        z = _dot(k_win, qt2) + b
        m = jnp.maximum(jnp.max(z, axis=0, keepdims=True), sink)
        p = jnp.exp(z - m)
        denom = jnp.sum(p, axis=0, keepdims=True) + jnp.exp(sink - m)
        ot = _dot(vt_win, p.astype(BF16)) * (1.0 / denom)
        o_t = jnp.where(lo_rows, ot[:, :WINDOW], ot[:, WINDOW:])
        o_ref[0, pl.ds(n * WINDOW, WINDOW), :] = o_t.T.astype(BF16)

    block(0, k_ref[0:WINDOW, :], vt_ref[0], bias(WINDOW, 0))
    b_full = bias(2 * WINDOW, WINDOW)

    def body(n, carry):
        start = pl.multiple_of((n - 1) * WINDOW, WINDOW)
        vt_win = jnp.concatenate([vt_ref[n - 1], vt_ref[n]], axis=1)
        block(n, k_ref[pl.ds(start, 2 * WINDOW), :], vt_win, b_full)
        return carry

    lax.fori_loop(1, n_blocks, body, 0)


def _swa(qta, ka, vta, slope_rows, sink_rows, batch, seq):
    tokens = ka.shape[0]
    nblk = seq // WINDOW
    return pl.pallas_call(
        _swa_kernel,
        grid=(batch, N_A_BLOCKS),
        in_specs=[
            pl.BlockSpec((1, nblk, LANES, WINDOW), lambda b, f: (f, b, 0, 0)),
            pl.BlockSpec((seq, LANES), lambda b, f: (b, 0)),
            pl.BlockSpec((nblk, LANES, WINDOW), lambda b, f: (b, 0, 0)),
            pl.BlockSpec((1, 1, 2 * WINDOW), lambda b, f: (f, 0, 0)),
            pl.BlockSpec((1, 1, 2 * WINDOW), lambda b, f: (f, 0, 0)),
        ],
        out_specs=pl.BlockSpec((1, seq, LANES), lambda b, f: (f, b, 0)),
        out_shape=jax.ShapeDtypeStruct((N_A_BLOCKS, tokens, LANES), BF16),
        compiler_params=_params("parallel", "parallel"),
        name="swa_attn",
    )(qta, ka, vta, slope_rows, sink_rows)


def _moba_kernel(qt_ref, k_ref, vt_ref, slope_ref, o_ref, s_scr, r_scr, b_scr):
    nb = qt_ref.shape[1]
    blk = MOBA_BLOCK
    key_pos = lax.broadcasted_iota(jnp.int32, (blk, blk), 0)
    qry_pos = lax.broadcasted_iota(jnp.int32, (blk, blk), 1)
    rel = (qry_pos - key_pos).astype(F32)
    feat = lax.broadcasted_iota(jnp.int32, (LANES, blk), 0)
    head_rows = (feat < HEAD_DIM, feat >= HEAD_DIM)
    blk_row = lax.broadcasted_iota(jnp.int32, (nb, blk), 0)

    for h in range(HEADS_PER_LANE_BLOCK):
        off = -slope_ref[0, h] * rel
        b_scr[h, 0] = off
        b_scr[h, 1] = jnp.where(qry_pos >= key_pos, off, NEG_INF)

    tok = lax.broadcasted_iota(jnp.int32, (nb, nb * blk), 1)
    blk_start = lax.broadcasted_iota(jnp.int32, (nb, nb * blk), 0) * blk
    avg = jnp.where((tok >= blk_start) & (tok < blk_start + blk), 1.0 / blk, 0.0)
    k_mean = _dot(avg.astype(BF16), k_ref[0])
    km_hi = k_mean.astype(BF16)
    km_lo = (k_mean - km_hi.astype(F32)).astype(BF16)

    def q_block(i, carry):
        qt = qt_ref[0, i]
        zero = jnp.zeros_like(qt)
        accs = []
        for h in range(HEADS_PER_LANE_BLOCK):
            qt_h = jnp.where(head_rows[h], qt, zero)
            slope = slope_ref[0, h]
            gate = _dot(km_hi, qt_h) + _dot(km_lo, qt_h)
            rank = jnp.zeros((nb, blk), jnp.int32)
            for jp in range(nb - 1):
                g_jp = gate[jp:jp + 1, :]
                beats = (g_jp > gate) | ((g_jp == gate) & (jp < blk_row))
                rank = rank + jnp.where(beats, jnp.where(jp < i, 1, 0), 0)
            selected = (blk_row < i) & (rank < MOBA_TOPK)
            block_off = -slope * (blk * (i - blk_row)).astype(F32)
            r_scr[h] = jnp.where(selected, block_off, NEG_INF)

            k_own = k_ref[0, pl.ds(pl.multiple_of(i * blk, blk), blk), :]
            z_own = _dot(k_own, qt_h) + b_scr[h, 1]
            s_scr[h, i] = z_own
            m_own = jnp.max(z_own, axis=0, keepdims=True)

            def scores(j, m, qt_h=qt_h, h=h):
                k_j = k_ref[0, pl.ds(pl.multiple_of(j * blk, blk), blk), :]
                z = _dot(k_j, qt_h) + b_scr[h, 0]
                s_scr[h, j] = z
                return jnp.maximum(m, jnp.max(z, axis=0, keepdims=True) + r_scr[h, pl.ds(j, 1), :])

            m = lax.fori_loop(0, i, scores, m_own)

            def values(j, acc, m=m, h=h):
                p = jnp.exp(s_scr[h, j] - (m - r_scr[h, pl.ds(j, 1), :]))
                vt = vt_ref[0, j]
                vt1 = jnp.where(head_rows[h], vt, jnp.ones_like(vt))
                return acc + _dot(vt1, p.astype(BF16))

            acc = lax.fori_loop(0, i, values, jnp.zeros((LANES, blk), F32))
            p_own = jnp.exp(s_scr[h, i] - m)
            vt = vt_ref[0, i]
            vt1 = jnp.where(head_rows[h], vt, jnp.ones_like(vt))
            accs.append(acc + _dot(vt1, p_own.astype(BF16)))

        denom0 = accs[0][HEAD_DIM:HEAD_DIM + 1, :]
        denom1 = accs[1][0:1, :]
        o_t = jnp.where(head_rows[0], accs[0] * (1.0 / denom0), accs[1] * (1.0 / denom1))
        o_ref[0, pl.ds(pl.multiple_of(i * blk, blk), blk), :] = o_t.T.astype(BF16)
        return carry

    lax.fori_loop(0, nb, q_block, 0)


def _moba(qtb, kb, vtb, slope_rows, batch, seq):
    tokens = kb.shape[1]
    nb = seq // MOBA_BLOCK
    return pl.pallas_call(
        _moba_kernel,
        grid=(batch, N_B_BLOCKS),
        in_specs=[
            pl.BlockSpec((1, nb, LANES, MOBA_BLOCK), lambda b, f: (f, b, 0, 0)),
            pl.BlockSpec((1, seq, LANES), lambda b, f: (f, b, 0)),
            pl.BlockSpec((1, nb, LANES, MOBA_BLOCK), lambda b, f: (f, b, 0, 0)),
            pl.BlockSpec((1, HEADS_PER_LANE_BLOCK, 1, MOBA_BLOCK), lambda b, f: (f, 0, 0, 0)),
        ],
        out_specs=pl.BlockSpec((1, seq, LANES), lambda b, f: (f, b, 0)),
        out_shape=jax.ShapeDtypeStruct((N_B_BLOCKS, tokens, LANES), BF16),
        scratch_shapes=[
            pltpu.VMEM((HEADS_PER_LANE_BLOCK, nb, MOBA_BLOCK, MOBA_BLOCK), F32),
            pltpu.VMEM((HEADS_PER_LANE_BLOCK, nb, MOBA_BLOCK), F32),
            pltpu.VMEM((HEADS_PER_LANE_BLOCK, 2, MOBA_BLOCK, MOBA_BLOCK), F32),
        ],
        compiler_params=_params("parallel", "parallel"),
        name="moba_attn",
    )(qtb, kb, vtb, slope_rows)


def _mixout_kernel(oa_ref, ob_ref, gg_ref, x_ref, mod_ref, g_ref, woa_ref, wob_ref, wout_ref, o_ref):
    d = x_ref.shape[1]
    oa = jnp.concatenate([oa_ref[f] for f in range(N_A_BLOCKS)], axis=1)
    ob = jnp.concatenate([ob_ref[f] for f in range(N_B_BLOCKS)], axis=1)
    gg = gg_ref[...].astype(F32)
    merged = (jax.nn.sigmoid(gg[:, :d]) * _dot(oa, woa_ref[...])
              + jax.nn.sigmoid(gg[:, d:]) * _dot(ob, wob_ref[...]))
    y = _dot(merged.astype(BF16), wout_ref[...])
    o_ref[...] = x_ref[...] + mod_ref[0, 2:3, :] * _rms(y, g_ref[...])


def _mixout(oa, ob, gg, x2, mod, g, woa, wob, wout, seq):
    tokens, d = x2.shape
    tm = 512
    tiles_per_seq = seq // tm
    const = lambda i: (0, 0)
    return pl.pallas_call(
        _mixout_kernel,
        grid=(tokens // tm,),
        in_specs=[
            pl.BlockSpec((N_A_BLOCKS, tm, LANES), lambda i: (0, i, 0)),
            pl.BlockSpec((N_B_BLOCKS, tm, LANES), lambda i: (0, i, 0)),
            pl.BlockSpec((tm, 2 * d), lambda i: (i, 0)),
            pl.BlockSpec((tm, d), lambda i: (i, 0)),
            pl.BlockSpec((1, N_MOD, d), lambda i: (i // tiles_per_seq, 0, 0)),
            pl.BlockSpec((1, d), const),
            pl.BlockSpec(woa.shape, const),
            pl.BlockSpec(wob.shape, const),
            pl.BlockSpec(wout.shape, const),
        ],
        out_specs=pl.BlockSpec((tm, d), lambda i: (i, 0)),
        out_shape=jax.ShapeDtypeStruct((tokens, d), F32),
        compiler_params=_params("parallel"),
        name="mix_out",
    )(oa, ob, gg, x2, mod, g, woa, wob, wout)


def _ffn_kernel(x_ref, mod_ref, gpre_ref, gpost_ref, wgu_ref, wd_ref, o_ref, act_scr):
    d_ff = wd_ref.shape[0]
    x = x_ref[...]
    h = (_rms(x, gpre_ref[...]) * (1.0 + mod_ref[0, 4:5, :]) + mod_ref[0, 3:4, :]).astype(BF16)
    c0 = 0
    while c0 < d_ff:
        c1 = min(c0 + 512, d_ff)
        gate = _dot(h, wgu_ref[:, c0:c1])
        up = _dot(h, wgu_ref[:, d_ff + c0:d_ff + c1])
        act_scr[:, c0:c1] = (gate * jax.nn.sigmoid(gate) * up).astype(BF16)
        c0 = c1
    y = _dot(act_scr[...], wd_ref[...])
    o_ref[...] = x + mod_ref[0, 5:6, :] * _rms(y, gpost_ref[...])


def _ffn(x2, mod, gpre, gpost, wgu, wd, seq):
    tokens, d = x2.shape
    d_ff = wd.shape[0]
    tm = 512
    tiles_per_seq = seq // tm
    const = lambda i: (0, 0)
    resident = dict(pipeline_mode=pl.Buffered(1))
    return pl.pallas_call(
        _ffn_kernel,
        grid=(tokens // tm,),
        in_specs=[
            pl.BlockSpec((tm, d), lambda i: (i, 0)),
            pl.BlockSpec((1, N_MOD, d), lambda i: (i // tiles_per_seq, 0, 0)),
            pl.BlockSpec((1, d), const),
            pl.BlockSpec((1, d), const),
            pl.BlockSpec(wgu.shape, const, **resident),
            pl.BlockSpec(wd.shape, const, **resident),
        ],
        out_specs=pl.BlockSpec((tm, d), lambda i: (i, 0)),
        out_shape=jax.ShapeDtypeStruct((tokens, d), F32),
        scratch_shapes=[pltpu.VMEM((tm, d_ff), BF16)],
        compiler_params=_params("parallel"),
        name="ffn",
    )(x2, mod, gpre, gpost, wgu, wd)


def _alibi_slopes():
    i = np.arange(1, N_ALIBI + 1, dtype=np.float32)
    s = (2.0 ** (-8.0 * i / N_ALIBI)).astype(np.float32)
    return s[:A_Q_HEADS], s[A_Q_HEADS:]


def _split_w_in(w_in_l, d):
    widths = (A_Q_W, A_KV_W, A_KV_W, B_W, B_W, B_W, d, d)
    offs = np.concatenate([[0], np.cumsum(widths)])
    a_q, a_k, a_v, b_q, b_k, b_v, g_a, g_b = (w_in_l[:, offs[n]:offs[n + 1]] for n in range(8))
    a_q = jnp.concatenate([a_q[:, h * HEAD_DIM:(h + 1) * HEAD_DIM] for h in A_HEAD_ORDER], axis=1)
    w_std = jnp.concatenate([g_a, g_b, b_k, a_k], axis=1).astype(BF16)
    w_t = jnp.concatenate([a_q, b_q, b_v, a_v], axis=1).T.astype(BF16)
    return w_std, w_t


def kernel(x, c, ada_w, ada_b, norm_pre_mix, norm_post_mix, w_in, attn_sinks, w_o_a, w_o_b, w_out,
           norm_pre_ffn, norm_post_ffn, w_gate_up, w_down):
    batch, seq, d = x.shape
    depth = ada_w.shape[0]
    assert seq % 512 == 0 and d % LANES == 0

    slopes_a, slopes_b = _alibi_slopes()
    a_order = np.asarray(A_HEAD_ORDER).reshape(N_A_BLOCKS, HEADS_PER_LANE_BLOCK)
    slope_a_rows = jnp.asarray(np.repeat(slopes_a[a_order], WINDOW, axis=1)[:, None, :])
    slope_b_rows = jnp.asarray(np.broadcast_to(
        slopes_b.reshape(N_B_BLOCKS, HEADS_PER_LANE_BLOCK, 1, 1),
        (N_B_BLOCKS, HEADS_PER_LANE_BLOCK, 1, MOBA_BLOCK)))

    mod_all = _ada(c, ada_w, ada_b).reshape(depth, batch, N_MOD, d)
    x2 = x.reshape(batch * seq, d)
    for l in range(depth):
        mod = mod_all[l]
        w_std, w_t = _split_w_in(w_in[l], d)
        sink_rows = jnp.repeat(attn_sinks[l][a_order], WINDOW, axis=1)[:, None, :]
        woa = jnp.concatenate([w_o_a[l][h * HEAD_DIM:(h + 1) * HEAD_DIM] for h in A_HEAD_ORDER],
                              axis=0).astype(BF16)
        gg, kb, ka, qta, qtb, vtb, vta = _inproj(x2, mod, norm_pre_mix[l][None], w_std, w_t, seq)
        oa = _swa(qta, ka, vta, slope_a_rows, sink_rows, batch, seq)
        ob = _moba(qtb, kb, vtb, slope_b_rows, batch, seq)
        x2 = _mixout(oa, ob, gg, x2, mod, norm_post_mix[l][None], woa, w_o_b[l].astype(BF16),
                     w_out[l].astype(BF16), seq)
        x2 = _ffn(x2, mod, norm_pre_ffn[l][None], norm_post_ffn[l][None],
                  w_gate_up[l].astype(BF16), w_down[l].astype(BF16), seq)
    return x2.reshape(batch, seq, d)
```

```python
import functools

import numpy as np
import jax
import jax.numpy as jnp
from jax import lax
from jax.experimental import pallas as pl
from jax.experimental.pallas import tpu as pltpu

HEAD_DIM = 64
A_Q_HEADS = 8
A_KV_HEADS = 2
A_GROUP = A_Q_HEADS // A_KV_HEADS
WINDOW = 128
B_HEADS = 8
MOBA_BLOCK = 256
MOBA_TOPK = 3
N_ALIBI = A_Q_HEADS + B_HEADS
N_MOD = 6
EPS = 1e-6

LANES = 128
HEADS_PER_LANE_BLOCK = LANES // HEAD_DIM
A_Q_W = A_Q_HEADS * HEAD_DIM
A_KV_W = A_KV_HEADS * HEAD_DIM
B_W = B_HEADS * HEAD_DIM
N_A_BLOCKS = A_Q_W // LANES
N_B_BLOCKS = B_W // LANES
VMEM_LIMIT_BYTES = 56 * 1024 * 1024

A_HEAD_ORDER = tuple(h for f in range(N_A_BLOCKS) for h in (f, A_GROUP + f))

F32 = jnp.float32
BF16 = jnp.bfloat16
NEG_INF = float("-inf")


def _dot(a, b):
    return jnp.dot(a, b, preferred_element_type=F32)


def _dot_nt(a, b):
    return lax.dot_general(a, b, (((1,), (1,)), ((), ())), preferred_element_type=F32)


def _rms(x, g):
    return x * lax.rsqrt(jnp.mean(x * x, axis=-1, keepdims=True) + EPS) * g


def _params(*semantics):
    return pltpu.CompilerParams(dimension_semantics=semantics, vmem_limit_bytes=VMEM_LIMIT_BYTES)


def _ada_kernel(c_ref, w_ref, b_ref, o_ref):
    c = c_ref[...]
    cond = c * jax.nn.sigmoid(c)
    w = w_ref[0]
    c_hi = cond.astype(BF16)
    c_lo = (cond - c_hi.astype(F32)).astype(BF16)
    w_hi = w.astype(BF16)
    w_lo = (w - w_hi.astype(F32)).astype(BF16)
    o_ref[0] = _dot(c_hi, w_hi) + _dot(c_hi, w_lo) + _dot(c_lo, w_hi) + b_ref[0]


def _ada(c, ada_w, ada_b):
    depth, d, n = ada_w.shape
    batch = c.shape[0]
    tn = 1024
    return pl.pallas_call(
        _ada_kernel,
        grid=(depth, n // tn),
        in_specs=[
            pl.BlockSpec((batch, d), lambda l, j: (0, 0)),
            pl.BlockSpec((1, d, tn), lambda l, j: (l, 0, j)),
            pl.BlockSpec((1, 1, tn), lambda l, j: (l, 0, j)),
        ],
        out_specs=pl.BlockSpec((1, batch, tn), lambda l, j: (l, 0, j)),
        out_shape=jax.ShapeDtypeStruct((depth, batch, n), F32),
        compiler_params=_params("parallel", "parallel"),
        name="ada_mod",
    )(c, ada_w, ada_b.reshape(depth, 1, n))


def _inproj_kernel(x_ref, mod_ref, g_ref, wstd_ref, wt_ref,
                   gg_ref, kb_ref, ka_ref, qta_ref, qtb_ref, vtb_ref, vta_ref, *, d_model):
    tm = x_ref.shape[0]
    x = x_ref[...]
    h = _rms(x, g_ref[...]) * (1.0 + mod_ref[0, 1:2, :]) + mod_ref[0, 0:1, :]
    hb = h.astype(BF16)

    n_gate = 2 * d_model
    for c0 in range(0, n_gate, 512):
        gg_ref[:, c0:c0 + 512] = _dot(hb, wstd_ref[:, c0:c0 + 512]).astype(BF16)
    kb = _dot(hb, wstd_ref[:, n_gate:n_gate + B_W])
    for f in range(N_B_BLOCKS):
        kb_ref[f] = kb[:, f * LANES:(f + 1) * LANES].astype(BF16)
    ka_ref[...] = _dot(hb, wstd_ref[:, n_gate + B_W:n_gate + B_W + A_KV_W]).astype(BF16)

    scale = HEAD_DIM ** -0.5
    qa = _dot_nt(wt_ref[0:A_Q_W, :], hb) * scale
    for f in range(N_A_BLOCKS):
        for t in range(tm // WINDOW):
            qta_ref[f, t] = qa[f * LANES:(f + 1) * LANES, t * WINDOW:(t + 1) * WINDOW].astype(BF16)
    qb = _dot_nt(wt_ref[A_Q_W:A_Q_W + B_W, :], hb) * scale
    for f in range(N_B_BLOCKS):
        for t in range(tm // MOBA_BLOCK):
            qtb_ref[f, t] = qb[f * LANES:(f + 1) * LANES, t * MOBA_BLOCK:(t + 1) * MOBA_BLOCK].astype(BF16)
    vb = _dot_nt(wt_ref[A_Q_W + B_W:A_Q_W + 2 * B_W, :], hb)
    for f in range(N_B_BLOCKS):
        for t in range(tm // MOBA_BLOCK):
            vtb_ref[f, t] = vb[f * LANES:(f + 1) * LANES, t * MOBA_BLOCK:(t + 1) * MOBA_BLOCK].astype(BF16)
    va = _dot_nt(wt_ref[A_Q_W + 2 * B_W:A_Q_W + 2 * B_W + A_KV_W, :], hb)
    for t in range(tm // WINDOW):
        vta_ref[t] = va[:, t * WINDOW:(t + 1) * WINDOW].astype(BF16)


def _inproj(x2, mod, g, w_std, w_t, seq):
    tokens, d = x2.shape
    tm = 512
    tiles_per_seq = seq // tm
    n_std = w_std.shape[1]
    n_t = w_t.shape[0]
    const = lambda i: (0, 0)
    out_shape = (
        jax.ShapeDtypeStruct((tokens, 2 * d), BF16),
        jax.ShapeDtypeStruct((N_B_BLOCKS, tokens, LANES), BF16),
        jax.ShapeDtypeStruct((tokens, LANES), BF16),
        jax.ShapeDtypeStruct((N_A_BLOCKS, tokens // WINDOW, LANES, WINDOW), BF16),
        jax.ShapeDtypeStruct((N_B_BLOCKS, tokens // MOBA_BLOCK, LANES, MOBA_BLOCK), BF16),
        jax.ShapeDtypeStruct((N_B_BLOCKS, tokens // MOBA_BLOCK, LANES, MOBA_BLOCK), BF16),
        jax.ShapeDtypeStruct((tokens // WINDOW, LANES, WINDOW), BF16),
    )
    out_specs = (
        pl.BlockSpec((tm, 2 * d), lambda i: (i, 0)),
        pl.BlockSpec((N_B_BLOCKS, tm, LANES), lambda i: (0, i, 0)),
        pl.BlockSpec((tm, LANES), lambda i: (i, 0)),
        pl.BlockSpec((N_A_BLOCKS, tm // WINDOW, LANES, WINDOW), lambda i: (0, i, 0, 0)),
        pl.BlockSpec((N_B_BLOCKS, tm // MOBA_BLOCK, LANES, MOBA_BLOCK), lambda i: (0, i, 0, 0)),
        pl.BlockSpec((N_B_BLOCKS, tm // MOBA_BLOCK, LANES, MOBA_BLOCK), lambda i: (0, i, 0, 0)),
        pl.BlockSpec((tm // WINDOW, LANES, WINDOW), lambda i: (i, 0, 0)),
    )
    return pl.pallas_call(
        functools.partial(_inproj_kernel, d_model=d),
        grid=(tokens // tm,),
        in_specs=[
            pl.BlockSpec((tm, d), lambda i: (i, 0)),
            pl.BlockSpec((1, N_MOD, d), lambda i: (i // tiles_per_seq, 0, 0)),
            pl.BlockSpec((1, d), const),
            pl.BlockSpec((d, n_std), const),
            pl.BlockSpec((n_t, d), const),
        ],
        out_specs=out_specs,
        out_shape=out_shape,
        compiler_params=_params("parallel"),
        name="in_proj",
    )(x2, mod, g, w_std, w_t)


def _swa_kernel(qt_ref, k_ref, vt_ref, slope_ref, sink_ref, o_ref):
    n_blocks = qt_ref.shape[1]
    slope = slope_ref[0]
    sink = sink_ref[0]
    lo_rows = lax.broadcasted_iota(jnp.int32, (LANES, WINDOW), 0) < HEAD_DIM

    def bias(n_keys, offset):
        r = lax.broadcasted_iota(jnp.int32, (n_keys, 2 * WINDOW), 0)
        col = lax.broadcasted_iota(jnp.int32, (n_keys, 2 * WINDOW), 1)
        qpos = jnp.where(col >= WINDOW, col - WINDOW, col)
        dist = offset + qpos - r
        valid = (dist >= 0) & (dist < WINDOW)
        return jnp.where(valid, -slope * dist.astype(F32), NEG_INF)

    def block(n, k_win, vt_win, b):
        qt = qt_ref[0, n]
        zero = jnp.zeros_like(qt)
        qt2 = jnp.concatenate([jnp.where(lo_rows, qt, zero), jnp.where(lo_rows, zero, qt)], axis=1)
        z = _dot(k_win, qt2) + b
        m = jnp.maximum(jnp.max(z, axis=0, keepdims=True), sink)
        p = jnp.exp(z - m)
        denom = jnp.sum(p, axis=0, keepdims=True) + jnp.exp(sink - m)
        ot = _dot(vt_win, p.astype(BF16)) * (1.0 / denom)
        o_t = jnp.where(lo_rows, ot[:, :WINDOW], ot[:, WINDOW:])
        o_ref[0, n * WINDOW:(n + 1) * WINDOW, :] = o_t.T.astype(BF16)

    block(0, k_ref[0:WINDOW, :], vt_ref[0], bias(WINDOW, 0))
    b_full = bias(2 * WINDOW, WINDOW)

    for n in range(1, n_blocks):
        vt_win = jnp.concatenate([vt_ref[n - 1], vt_ref[n]], axis=1)
        block(n, k_ref[(n - 1) * WINDOW:(n + 1) * WINDOW, :], vt_win, b_full)


def _swa(qta, ka, vta, slope_rows, sink_rows, batch, seq):
    tokens = ka.shape[0]
    nblk = seq // WINDOW
    return pl.pallas_call(
        _swa_kernel,
        grid=(batch, N_A_BLOCKS),
        in_specs=[
            pl.BlockSpec((1, nblk, LANES, WINDOW), lambda b, f: (f, b, 0, 0)),
            pl.BlockSpec((seq, LANES), lambda b, f: (b, 0)),
            pl.BlockSpec((nblk, LANES, WINDOW), lambda b, f: (b, 0, 0)),
            pl.BlockSpec((1, 1, 2 * WINDOW), lambda b, f: (f, 0, 0)),
            pl.BlockSpec((1, 1, 2 * WINDOW), lambda b, f: (f, 0, 0)),
        ],
        out_specs=pl.BlockSpec((1, seq, LANES), lambda b, f: (f, b, 0)),
        out_shape=jax.ShapeDtypeStruct((N_A_BLOCKS, tokens, LANES), BF16),
        compiler_params=_params("parallel", "parallel"),
        name="swa_attn",
    )(qta, ka, vta, slope_rows, sink_rows)


def _moba_kernel(qt_ref, k_ref, vt_ref, slope_ref, o_ref, bias_scr):
    nb = qt_ref.shape[1]
    blk = MOBA_BLOCK
    seq = nb * blk
    feat = lax.broadcasted_iota(jnp.int32, (LANES, blk), 0)
    head_rows = (feat < HEAD_DIM, feat >= HEAD_DIM)
    blk_row = lax.broadcasted_iota(jnp.int32, (nb, blk), 0)

    dist = (seq - blk
            + lax.broadcasted_iota(jnp.int32, (seq, blk), 1)
            - lax.broadcasted_iota(jnp.int32, (seq, blk), 0))
    for h in range(HEADS_PER_LANE_BLOCK):
        bias_scr[h] = jnp.where(dist >= 0, -slope_ref[0, h] * dist.astype(F32), NEG_INF)

    tok = lax.broadcasted_iota(jnp.int32, (nb, seq), 1)
    blk_start = lax.broadcasted_iota(jnp.int32, (nb, seq), 0) * blk
    avg = jnp.where((tok >= blk_start) & (tok < blk_start + blk), 1.0 / blk, 0.0)
    k_mean = _dot(avg.astype(BF16), k_ref[0])
    km_hi = k_mean.astype(BF16)
    km_lo = (k_mean - km_hi.astype(F32)).astype(BF16)

    for i in range(nb):
        qt = qt_ref[0, i]
        zero = jnp.zeros_like(qt)
        n_keys = blk * (i + 1)
        vt = jnp.concatenate([vt_ref[0, j] for j in range(i + 1)], axis=1)
        accs = []
        for h in range(HEADS_PER_LANE_BLOCK):
            qt_h = jnp.where(head_rows[h], qt, zero)
            gated = i > MOBA_TOPK
            if gated:
                gate = _dot(km_hi, qt_h) + _dot(km_lo, qt_h)
                rank = jnp.zeros((nb, blk), jnp.int32)
                for jp in range(i):
                    g_jp = gate[jp:jp + 1, :]
                    beats = (g_jp > gate) | ((g_jp == gate) & (jp < blk_row))
                    rank = rank + jnp.where(beats, 1, 0)
                unselected = jnp.where(rank < MOBA_TOPK, 0.0, NEG_INF)
            z = _dot(k_ref[0, 0:n_keys, :], qt_h) + bias_scr[h, seq - n_keys:seq, :]
            if gated:
                m = jnp.max(z[i * blk:n_keys], axis=0, keepdims=True)
                for j in range(i):
                    m = jnp.maximum(m, jnp.max(z[j * blk:(j + 1) * blk], axis=0, keepdims=True)
                                    + unselected[j:j + 1, :])
                ps = [jnp.exp(z[j * blk:(j + 1) * blk] - (m - unselected[j:j + 1, :])) for j in range(i)]
                ps.append(jnp.exp(z[i * blk:n_keys] - m))
                p = jnp.concatenate([pj.astype(BF16) for pj in ps], axis=0)
            else:
                m = jnp.max(z, axis=0, keepdims=True)
                p = jnp.exp(z - m).astype(BF16)
            lo_rows = lax.broadcasted_iota(jnp.int32, vt.shape, 0) < HEAD_DIM
            own_rows = lo_rows if h == 0 else jnp.logical_not(lo_rows)
            vt1 = jnp.where(own_rows, vt, jnp.ones_like(vt))
            accs.append(_dot(vt1, p))

        denom0 = accs[0][HEAD_DIM:HEAD_DIM + 1, :]
        denom1 = accs[1][0:1, :]
        o_t = jnp.where(head_rows[0], accs[0] * (1.0 / denom0), accs[1] * (1.0 / denom1))
        o_ref[0, i * blk:(i + 1) * blk, :] = o_t.T.astype(BF16)


def _moba(qtb, kb, vtb, slope_rows, batch, seq):
    tokens = kb.shape[1]
    nb = seq // MOBA_BLOCK
    return pl.pallas_call(
        _moba_kernel,
        grid=(batch, N_B_BLOCKS),
        in_specs=[
            pl.BlockSpec((1, nb, LANES, MOBA_BLOCK), lambda b, f: (f, b, 0, 0)),
            pl.BlockSpec((1, seq, LANES), lambda b, f: (f, b, 0)),
            pl.BlockSpec((1, nb, LANES, MOBA_BLOCK), lambda b, f: (f, b, 0, 0)),
            pl.BlockSpec((1, HEADS_PER_LANE_BLOCK, 1, MOBA_BLOCK), lambda b, f: (f, 0, 0, 0)),
        ],
        out_specs=pl.BlockSpec((1, seq, LANES), lambda b, f: (f, b, 0)),
        out_shape=jax.ShapeDtypeStruct((N_B_BLOCKS, tokens, LANES), BF16),
        scratch_shapes=[pltpu.VMEM((HEADS_PER_LANE_BLOCK, seq, MOBA_BLOCK), F32)],
        compiler_params=_params("parallel", "parallel"),
        name="moba_attn",
    )(qtb, kb, vtb, slope_rows)


def _mixout_kernel(oa_ref, ob_ref, gg_ref, x_ref, mod_ref, g_ref, woa_ref, wob_ref, wout_ref, o_ref):
    d = x_ref.shape[1]
    oa = jnp.concatenate([oa_ref[f] for f in range(N_A_BLOCKS)], axis=1)
    ob = jnp.concatenate([ob_ref[f] for f in range(N_B_BLOCKS)], axis=1)
    gg = gg_ref[...].astype(F32)
    merged = (jax.nn.sigmoid(gg[:, :d]) * _dot(oa, woa_ref[...])
              + jax.nn.sigmoid(gg[:, d:]) * _dot(ob, wob_ref[...]))
    y = _dot(merged.astype(BF16), wout_ref[...])
    o_ref[...] = x_ref[...] + mod_ref[0, 2:3, :] * _rms(y, g_ref[...])


def _mixout(oa, ob, gg, x2, mod, g, woa, wob, wout, seq):
    tokens, d = x2.shape
    tm = 512
    tiles_per_seq = seq // tm
    const = lambda i: (0, 0)
    return pl.pallas_call(
        _mixout_kernel,
        grid=(tokens // tm,),
        in_specs=[
            pl.BlockSpec((N_A_BLOCKS, tm, LANES), lambda i: (0, i, 0)),
            pl.BlockSpec((N_B_BLOCKS, tm, LANES), lambda i: (0, i, 0)),
            pl.BlockSpec((tm, 2 * d), lambda i: (i, 0)),
            pl.BlockSpec((tm, d), lambda i: (i, 0)),
            pl.BlockSpec((1, N_MOD, d), lambda i: (i // tiles_per_seq, 0, 0)),
            pl.BlockSpec((1, d), const),
            pl.BlockSpec(woa.shape, const),
            pl.BlockSpec(wob.shape, const),
            pl.BlockSpec(wout.shape, const),
        ],
        out_specs=pl.BlockSpec((tm, d), lambda i: (i, 0)),
        out_shape=jax.ShapeDtypeStruct((tokens, d), F32),
        compiler_params=_params("parallel"),
        name="mix_out",
    )(oa, ob, gg, x2, mod, g, woa, wob, wout)


def _ffn_kernel(x_ref, mod_ref, gpre_ref, gpost_ref, wgu_ref, wd_ref, o_ref, act_scr):
    d_ff = wd_ref.shape[0]
    x = x_ref[...]
    h = (_rms(x, gpre_ref[...]) * (1.0 + mod_ref[0, 4:5, :]) + mod_ref[0, 3:4, :]).astype(BF16)
    c0 = 0
    while c0 < d_ff:
        c1 = min(c0 + 512, d_ff)
        gate = _dot(h, wgu_ref[:, c0:c1])
        up = _dot(h, wgu_ref[:, d_ff + c0:d_ff + c1])
        act_scr[:, c0:c1] = (gate * jax.nn.sigmoid(gate) * up).astype(BF16)
        c0 = c1
    y = _dot(act_scr[...], wd_ref[...])
    o_ref[...] = x + mod_ref[0, 5:6, :] * _rms(y, gpost_ref[...])


def _ffn(x2, mod, gpre, gpost, wgu, wd, seq):
    tokens, d = x2.shape
    d_ff = wd.shape[0]
    tm = 512
    tiles_per_seq = seq // tm
    const = lambda i: (0, 0)
    resident = dict(pipeline_mode=pl.Buffered(1))
    return pl.pallas_call(
        _ffn_kernel,
        grid=(tokens // tm,),
        in_specs=[
            pl.BlockSpec((tm, d), lambda i: (i, 0)),
            pl.BlockSpec((1, N_MOD, d), lambda i: (i // tiles_per_seq, 0, 0)),
            pl.BlockSpec((1, d), const),
            pl.BlockSpec((1, d), const),
            pl.BlockSpec(wgu.shape, const, **resident),
            pl.BlockSpec(wd.shape, const, **resident),
        ],
        out_specs=pl.BlockSpec((tm, d), lambda i: (i, 0)),
        out_shape=jax.ShapeDtypeStruct((tokens, d), F32),
        scratch_shapes=[pltpu.VMEM((tm, d_ff), BF16)],
        compiler_params=_params("parallel"),
        name="ffn",
    )(x2, mod, gpre, gpost, wgu, wd)


def _alibi_slopes():
    i = np.arange(1, N_ALIBI + 1, dtype=np.float32)
    s = (2.0 ** (-8.0 * i / N_ALIBI)).astype(np.float32)
    return s[:A_Q_HEADS], s[A_Q_HEADS:]


def _split_w_in(w_in_l, d):
    widths = (A_Q_W, A_KV_W, A_KV_W, B_W, B_W, B_W, d, d)
    offs = np.concatenate([[0], np.cumsum(widths)])
    a_q, a_k, a_v, b_q, b_k, b_v, g_a, g_b = (w_in_l[:, offs[n]:offs[n + 1]] for n in range(8))
    a_q = jnp.concatenate([a_q[:, h * HEAD_DIM:(h + 1) * HEAD_DIM] for h in A_HEAD_ORDER], axis=1)
    w_std = jnp.concatenate([g_a, g_b, b_k, a_k], axis=1).astype(BF16)
    w_t = jnp.concatenate([a_q, b_q, b_v, a_v], axis=1).T.astype(BF16)
    return w_std, w_t


def kernel(x, c, ada_w, ada_b, norm_pre_mix, norm_post_mix, w_in, attn_sinks, w_o_a, w_o_b, w_out,
           norm_pre_ffn, norm_post_ffn, w_gate_up, w_down):
    batch, seq, d = x.shape
    depth = ada_w.shape[0]
    assert seq % 512 == 0 and d % LANES == 0

    slopes_a, slopes_b = _alibi_slopes()
    a_order = np.asarray(A_HEAD_ORDER).reshape(N_A_BLOCKS, HEADS_PER_LANE_BLOCK)
    slope_a_rows = jnp.asarray(np.repeat(slopes_a[a_order], WINDOW, axis=1)[:, None, :])
    slope_b_rows = jnp.asarray(np.broadcast_to(
        slopes_b.reshape(N_B_BLOCKS, HEADS_PER_LANE_BLOCK, 1, 1),
        (N_B_BLOCKS, HEADS_PER_LANE_BLOCK, 1, MOBA_BLOCK)))

    mod_all = _ada(c, ada_w, ada_b).reshape(depth, batch, N_MOD, d)
    x2 = x.reshape(batch * seq, d)
    for l in range(depth):
        mod = mod_all[l]
        w_std, w_t = _split_w_in(w_in[l], d)
        sink_rows = jnp.repeat(attn_sinks[l][a_order], WINDOW, axis=1)[:, None, :]
        woa = jnp.concatenate([w_o_a[l][h * HEAD_DIM:(h + 1) * HEAD_DIM] for h in A_HEAD_ORDER],
                              axis=0).astype(BF16)
        gg, kb, ka, qta, qtb, vtb, vta = _inproj(x2, mod, norm_pre_mix[l][None], w_std, w_t, seq)
        oa = _swa(qta, ka, vta, slope_a_rows, sink_rows, batch, seq)
        ob = _moba(qtb, kb, vtb, slope_b_rows, batch, seq)
        x2 = _mixout(oa, ob, gg, x2, mod, norm_post_mix[l][None], woa, w_o_b[l].astype(BF16),
                     w_out[l].astype(BF16), seq)
        x2 = _ffn(x2, mod, norm_pre_ffn[l][None], norm_post_ffn[l][None],
                  w_gate_up[l].astype(BF16), w_down[l].astype(BF16), seq)
    return x2.reshape(batch, seq, d)
```

```python
import functools

import numpy as np
import jax
import jax.numpy as jnp
from jax import lax
from jax.experimental import pallas as pl
from jax.experimental.pallas import tpu as pltpu

HEAD_DIM = 64
A_Q_HEADS = 8
A_KV_HEADS = 2
A_GROUP = A_Q_HEADS // A_KV_HEADS
WINDOW = 128
B_HEADS = 8
MOBA_BLOCK = 256
MOBA_TOPK = 3
N_ALIBI = A_Q_HEADS + B_HEADS
N_MOD = 6
EPS = 1e-6

LANES = 128
HEADS_PER_LANE_BLOCK = LANES // HEAD_DIM
A_Q_W = A_Q_HEADS * HEAD_DIM
A_KV_W = A_KV_HEADS * HEAD_DIM
B_W = B_HEADS * HEAD_DIM
N_A_BLOCKS = A_Q_W // LANES
N_B_BLOCKS = B_W // LANES
VMEM_LIMIT_BYTES = 56 * 1024 * 1024

A_HEAD_ORDER = tuple(h for f in range(N_A_BLOCKS) for h in (f, A_GROUP + f))

F32 = jnp.float32
BF16 = jnp.bfloat16
NEG_INF = float("-inf")


def _dot(a, b):
    return jnp.dot(a, b, preferred_element_type=F32)


def _dot_nt(a, b):
    return lax.dot_general(a, b, (((1,), (1,)), ((), ())), preferred_element_type=F32)


def _rms(x, g):
    return x * lax.rsqrt(jnp.mean(x * x, axis=-1, keepdims=True) + EPS) * g


def _params(*semantics):
    return pltpu.CompilerParams(dimension_semantics=semantics, vmem_limit_bytes=VMEM_LIMIT_BYTES)


def _ada_kernel(c_ref, w_ref, b_ref, o_ref):
    c = c_ref[...]
    cond = c * jax.nn.sigmoid(c)
    w = w_ref[0]
    c_hi = cond.astype(BF16)
    c_lo = (cond - c_hi.astype(F32)).astype(BF16)
    w_hi = w.astype(BF16)
    w_lo = (w - w_hi.astype(F32)).astype(BF16)
    o_ref[0] = _dot(c_hi, w_hi) + _dot(c_hi, w_lo) + _dot(c_lo, w_hi) + b_ref[0]


def _ada(c, ada_w, ada_b):
    depth, d, n = ada_w.shape
    batch = c.shape[0]
    tn = 1024
    return pl.pallas_call(
        _ada_kernel,
        grid=(depth, n // tn),
        in_specs=[
            pl.BlockSpec((batch, d), lambda l, j: (0, 0)),
            pl.BlockSpec((1, d, tn), lambda l, j: (l, 0, j)),
            pl.BlockSpec((1, 1, tn), lambda l, j: (l, 0, j)),
        ],
        out_specs=pl.BlockSpec((1, batch, tn), lambda l, j: (l, 0, j)),
        out_shape=jax.ShapeDtypeStruct((depth, batch, n), F32),
        compiler_params=_params("parallel", "parallel"),
        name="ada_mod",
    )(c, ada_w, ada_b.reshape(depth, 1, n))


def _inproj_kernel(x_ref, mod_ref, g_ref, wstd_ref, wt_ref,
                   gg_ref, kb_ref, ka_ref, qta_ref, qtb_ref, vtb_ref, vta_ref, *, d_model):
    tm = x_ref.shape[0]
    x = x_ref[...]
    h = _rms(x, g_ref[...]) * (1.0 + mod_ref[0, 1:2, :]) + mod_ref[0, 0:1, :]
    hb = h.astype(BF16)

    n_gate = 2 * d_model
    for c0 in range(0, n_gate, 512):
        gg_ref[:, c0:c0 + 512] = _dot(hb, wstd_ref[:, c0:c0 + 512]).astype(BF16)
    kb = _dot(hb, wstd_ref[:, n_gate:n_gate + B_W])
    for f in range(N_B_BLOCKS):
        kb_ref[f] = kb[:, f * LANES:(f + 1) * LANES].astype(BF16)
    ka_ref[...] = _dot(hb, wstd_ref[:, n_gate + B_W:n_gate + B_W + A_KV_W]).astype(BF16)

    scale = HEAD_DIM ** -0.5
    qa = _dot_nt(wt_ref[0:A_Q_W, :], hb) * scale
    for f in range(N_A_BLOCKS):
        for t in range(tm // WINDOW):
            qta_ref[f, t] = qa[f * LANES:(f + 1) * LANES, t * WINDOW:(t + 1) * WINDOW].astype(BF16)
    qb = _dot_nt(wt_ref[A_Q_W:A_Q_W + B_W, :], hb) * scale
    for f in range(N_B_BLOCKS):
        for t in range(tm // MOBA_BLOCK):
            qtb_ref[f, t] = qb[f * LANES:(f + 1) * LANES, t * MOBA_BLOCK:(t + 1) * MOBA_BLOCK].astype(BF16)
    vb = _dot_nt(wt_ref[A_Q_W + B_W:A_Q_W + 2 * B_W, :], hb)
    for f in range(N_B_BLOCKS):
        for t in range(tm // MOBA_BLOCK):
            vtb_ref[f, t] = vb[f * LANES:(f + 1) * LANES, t * MOBA_BLOCK:(t + 1) * MOBA_BLOCK].astype(BF16)
    va = _dot_nt(wt_ref[A_Q_W + 2 * B_W:A_Q_W + 2 * B_W + A_KV_W, :], hb)
    for t in range(tm // WINDOW):
        vta_ref[t] = va[:, t * WINDOW:(t + 1) * WINDOW].astype(BF16)


def _inproj(x2, mod, g, w_std, w_t, seq):
    tokens, d = x2.shape
    tm = 512
    tiles_per_seq = seq // tm
    n_std = w_std.shape[1]
    n_t = w_t.shape[0]
    const = lambda i: (0, 0)
    out_shape = (
        jax.ShapeDtypeStruct((tokens, 2 * d), BF16),
        jax.ShapeDtypeStruct((N_B_BLOCKS, tokens, LANES), BF16),
        jax.ShapeDtypeStruct((tokens, LANES), BF16),
        jax.ShapeDtypeStruct((N_A_BLOCKS, tokens // WINDOW, LANES, WINDOW), BF16),
        jax.ShapeDtypeStruct((N_B_BLOCKS, tokens // MOBA_BLOCK, LANES, MOBA_BLOCK), BF16),
        jax.ShapeDtypeStruct((N_B_BLOCKS, tokens // MOBA_BLOCK, LANES, MOBA_BLOCK), BF16),
        jax.ShapeDtypeStruct((tokens // WINDOW, LANES, WINDOW), BF16),
    )
    out_specs = (
        pl.BlockSpec((tm, 2 * d), lambda i: (i, 0)),
        pl.BlockSpec((N_B_BLOCKS, tm, LANES), lambda i: (0, i, 0)),
        pl.BlockSpec((tm, LANES), lambda i: (i, 0)),
        pl.BlockSpec((N_A_BLOCKS, tm // WINDOW, LANES, WINDOW), lambda i: (0, i, 0, 0)),
        pl.BlockSpec((N_B_BLOCKS, tm // MOBA_BLOCK, LANES, MOBA_BLOCK), lambda i: (0, i, 0, 0)),
        pl.BlockSpec((N_B_BLOCKS, tm // MOBA_BLOCK, LANES, MOBA_BLOCK), lambda i: (0, i, 0, 0)),
        pl.BlockSpec((tm // WINDOW, LANES, WINDOW), lambda i: (i, 0, 0)),
    )
    return pl.pallas_call(
        functools.partial(_inproj_kernel, d_model=d),
        grid=(tokens // tm,),
        in_specs=[
            pl.BlockSpec((tm, d), lambda i: (i, 0)),
            pl.BlockSpec((1, N_MOD, d), lambda i: (i // tiles_per_seq, 0, 0)),
            pl.BlockSpec((1, d), const),
            pl.BlockSpec((d, n_std), const),
            pl.BlockSpec((n_t, d), const),
        ],
        out_specs=out_specs,
        out_shape=out_shape,
        compiler_params=_params("parallel"),
        name="in_proj",
    )(x2, mod, g, w_std, w_t)


def _swa_kernel(qt_ref, k_ref, vt_ref, slope_ref, sink_ref, o_ref, bias_scr):
    n_blocks = qt_ref.shape[1]
    slope = slope_ref[0]
    sink = sink_ref[0]
    lo_rows = lax.broadcasted_iota(jnp.int32, (LANES, WINDOW), 0) < HEAD_DIM

    def bias(n_keys, offset):
        r = lax.broadcasted_iota(jnp.int32, (n_keys, 2 * WINDOW), 0)
        col = lax.broadcasted_iota(jnp.int32, (n_keys, 2 * WINDOW), 1)
        qpos = jnp.where(col >= WINDOW, col - WINDOW, col)
        dist = offset + qpos - r
        valid = (dist >= 0) & (dist < WINDOW)
        return jnp.where(valid, -slope * dist.astype(F32), NEG_INF)

    bias_scr[...] = bias(2 * WINDOW, WINDOW)
    bias_first = bias(WINDOW, 0)

    def scores(n):
        qt = qt_ref[0, n]
        zero = jnp.zeros_like(qt)
        qt2 = jnp.concatenate([jnp.where(lo_rows, qt, zero), jnp.where(lo_rows, zero, qt)], axis=1)
        if n == 0:
            return _dot(k_ref[0:WINDOW, :], qt2) + bias_first
        return _dot(k_ref[(n - 1) * WINDOW:(n + 1) * WINDOW, :], qt2) + bias_scr[...]

    def softmax(z):
        m = jnp.maximum(jnp.max(z, axis=0, keepdims=True), sink)
        p = jnp.exp(z - m)
        denom = jnp.sum(p, axis=0, keepdims=True) + jnp.exp(sink - m)
        return p.astype(BF16), denom

    def finish(n, p, denom):
        vt_win = vt_ref[0] if n == 0 else jnp.concatenate([vt_ref[n - 1], vt_ref[n]], axis=1)
        ot = _dot(vt_win, p) * (1.0 / denom)
        o_t = jnp.where(lo_rows, ot[:, :WINDOW], ot[:, WINDOW:])
        o_ref[0, n * WINDOW:(n + 1) * WINDOW, :] = o_t.T.astype(BF16)

    z = pd = None
    for s in range(n_blocks + 2):
        z_new = scores(s) if s < n_blocks else None
        pd_new = softmax(z) if 1 <= s <= n_blocks else None
        if s >= 2:
            finish(s - 2, *pd)
        z, pd = z_new, pd_new


def _swa(qta, ka, vta, slope_rows, sink_rows, batch, seq):
    tokens = ka.shape[0]
    nblk = seq // WINDOW
    return pl.pallas_call(
        _swa_kernel,
        grid=(batch, N_A_BLOCKS),
        in_specs=[
            pl.BlockSpec((1, nblk, LANES, WINDOW), lambda b, f: (f, b, 0, 0)),
            pl.BlockSpec((seq, LANES), lambda b, f: (b, 0)),
            pl.BlockSpec((nblk, LANES, WINDOW), lambda b, f: (b, 0, 0)),
            pl.BlockSpec((1, 1, 2 * WINDOW), lambda b, f: (f, 0, 0)),
            pl.BlockSpec((1, 1, 2 * WINDOW), lambda b, f: (f, 0, 0)),
        ],
        out_specs=pl.BlockSpec((1, seq, LANES), lambda b, f: (f, b, 0)),
        out_shape=jax.ShapeDtypeStruct((N_A_BLOCKS, tokens, LANES), BF16),
        scratch_shapes=[pltpu.VMEM((2 * WINDOW, 2 * WINDOW), F32)],
        compiler_params=_params("parallel", "parallel"),
        name="swa_attn",
    )(qta, ka, vta, slope_rows, sink_rows)


def _moba_kernel(qt_ref, k_ref, vt_ref, slope_ref, o_ref, bias_scr, z_scr, p_scr, vt1_scr):
    nb = qt_ref.shape[1]
    blk = MOBA_BLOCK
    seq = nb * blk
    feat = lax.broadcasted_iota(jnp.int32, (LANES, blk), 0)
    head_rows = (feat < HEAD_DIM, feat >= HEAD_DIM)
    blk_row = lax.broadcasted_iota(jnp.int32, (nb, blk), 0)

    dist = (seq - blk
            + lax.broadcasted_iota(jnp.int32, (seq, blk), 1)
            - lax.broadcasted_iota(jnp.int32, (seq, blk), 0))
    for h in range(HEADS_PER_LANE_BLOCK):
        bias_scr[h] = jnp.where(dist >= 0, -slope_ref[0, h] * dist.astype(F32), NEG_INF)

    tok = lax.broadcasted_iota(jnp.int32, (nb, seq), 1)
    blk_start = lax.broadcasted_iota(jnp.int32, (nb, seq), 0) * blk
    avg = jnp.where((tok >= blk_start) & (tok < blk_start + blk), 1.0 / blk, 0.0)
    k_mean = _dot(avg.astype(BF16), k_ref[0])
    km_hi = k_mean.astype(BF16)
    km_lo = (k_mean - km_hi.astype(F32)).astype(BF16)

    for j in range(nb):
        vt = vt_ref[0, j]
        for h in range(HEADS_PER_LANE_BLOCK):
            vt1_scr[h, j] = jnp.where(head_rows[h], vt, jnp.ones_like(vt))

    tasks = [(i, h) for i in range(nb) for h in range(HEADS_PER_LANE_BLOCK)]
    state = [dict() for _ in tasks]

    def scores_begin(t):
        i, h = tasks[t]
        qt = qt_ref[0, i]
        qt_h = jnp.where(head_rows[h], qt, jnp.zeros_like(qt))
        st = state[t]
        st["qt"] = qt_h
        st["cm"] = []
        if i > MOBA_TOPK:
            gate = _dot(km_hi, qt_h) + _dot(km_lo, qt_h)
            rank = jnp.zeros((nb, blk), jnp.int32)
            for jp in range(i):
                g_jp = gate[jp:jp + 1, :]
                beats = (g_jp > gate) | ((g_jp == gate) & (jp < blk_row))
                rank = rank + jnp.where(beats, 1, 0)
            st["unsel"] = jnp.where(rank < MOBA_TOPK, 0.0, NEG_INF)

    def scores_tile(t, j):
        i, h = tasks[t]
        st = state[t]
        row0 = seq - blk * (i + 1) + blk * j
        z = _dot(k_ref[0, j * blk:(j + 1) * blk, :], st["qt"]) + bias_scr[h, row0:row0 + blk, :]
        z_scr[t % 2, j] = z
        cm = jnp.max(z, axis=0, keepdims=True)
        if "unsel" in st and j < i:
            cm = cm + st["unsel"][j:j + 1, :]
        st["cm"].append(cm)

    def scores_end(t):
        i, h = tasks[t]
        st = state[t]
        m = functools.reduce(jnp.maximum, st["cm"])
        st["shift"] = [m - st["unsel"][j:j + 1, :] if ("unsel" in st and j < i) else m
                       for j in range(i + 1)]

    def probs_tile(t, j):
        p = jnp.exp(z_scr[t % 2, j] - state[t]["shift"][j])
        p_scr[t % 2, j] = p.astype(BF16)

    def values_tile(t, j):
        i, h = tasks[t]
        st = state[t]
        part = _dot(vt1_scr[h, j], p_scr[t % 2, j])
        st["acc"] = part if j == 0 else st["acc"] + part

    def values_end(t):
        i, h = tasks[t]
        if h != HEADS_PER_LANE_BLOCK - 1:
            return
        acc0, acc1 = state[t - 1]["acc"], state[t]["acc"]
        denom0 = acc0[HEAD_DIM:HEAD_DIM + 1, :]
        denom1 = acc1[0:1, :]
        o_t = jnp.where(head_rows[0], acc0 * (1.0 / denom0), acc1 * (1.0 / denom1))
        o_ref[0, i * blk:(i + 1) * blk, :] = o_t.T.astype(BF16)
        state[t - 1].clear()
        state[t].clear()

    n_tasks = len(tasks)
    for step in range(n_tasks + 2):
        t_s, t_p, t_v = step, step - 1, step - 2
        live = [t for t in (t_s, t_p, t_v) if 0 <= t < n_tasks]
        if 0 <= t_s < n_tasks:
            scores_begin(t_s)
        for j in range(max(tasks[t][0] + 1 for t in live)):
            if 0 <= t_s < n_tasks and j <= tasks[t_s][0]:
                scores_tile(t_s, j)
            if 0 <= t_p < n_tasks and j <= tasks[t_p][0]:
                probs_tile(t_p, j)
            if 0 <= t_v < n_tasks and j <= tasks[t_v][0]:
                values_tile(t_v, j)
        if 0 <= t_s < n_tasks:
            scores_end(t_s)
        if 0 <= t_v < n_tasks:
            values_end(t_v)


def _moba(qtb, kb, vtb, slope_rows, batch, seq):
    tokens = kb.shape[1]
    nb = seq // MOBA_BLOCK
    return pl.pallas_call(
        _moba_kernel,
        grid=(batch, N_B_BLOCKS),
        in_specs=[
            pl.BlockSpec((1, nb, LANES, MOBA_BLOCK), lambda b, f: (f, b, 0, 0)),
            pl.BlockSpec((1, seq, LANES), lambda b, f: (f, b, 0)),
            pl.BlockSpec((1, nb, LANES, MOBA_BLOCK), lambda b, f: (f, b, 0, 0)),
            pl.BlockSpec((1, HEADS_PER_LANE_BLOCK, 1, MOBA_BLOCK), lambda b, f: (f, 0, 0, 0)),
        ],
        out_specs=pl.BlockSpec((1, seq, LANES), lambda b, f: (f, b, 0)),
        out_shape=jax.ShapeDtypeStruct((N_B_BLOCKS, tokens, LANES), BF16),
        scratch_shapes=[
            pltpu.VMEM((HEADS_PER_LANE_BLOCK, seq, MOBA_BLOCK), F32),
            pltpu.VMEM((2, nb, MOBA_BLOCK, MOBA_BLOCK), F32),
            pltpu.VMEM((2, nb, MOBA_BLOCK, MOBA_BLOCK), BF16),
            pltpu.VMEM((HEADS_PER_LANE_BLOCK, nb, LANES, MOBA_BLOCK), BF16),
        ],
        compiler_params=_params("parallel", "parallel"),
        name="moba_attn",
    )(qtb, kb, vtb, slope_rows)


def _mixout_kernel(oa_ref, ob_ref, gg_ref, x_ref, mod_ref, g_ref, woa_ref, wob_ref, wout_ref, o_ref):
    d = x_ref.shape[1]
    oa = jnp.concatenate([oa_ref[f] for f in range(N_A_BLOCKS)], axis=1)
    ob = jnp.concatenate([ob_ref[f] for f in range(N_B_BLOCKS)], axis=1)
    gg = gg_ref[...].astype(F32)
    merged = (jax.nn.sigmoid(gg[:, :d]) * _dot(oa, woa_ref[...])
              + jax.nn.sigmoid(gg[:, d:]) * _dot(ob, wob_ref[...]))
    y = _dot(merged.astype(BF16), wout_ref[...])
    o_ref[...] = x_ref[...] + mod_ref[0, 2:3, :] * _rms(y, g_ref[...])


def _mixout(oa, ob, gg, x2, mod, g, woa, wob, wout, seq):
    tokens, d = x2.shape
    tm = 512
    tiles_per_seq = seq // tm
    const = lambda i: (0, 0)
    return pl.pallas_call(
        _mixout_kernel,
        grid=(tokens // tm,),
        in_specs=[
            pl.BlockSpec((N_A_BLOCKS, tm, LANES), lambda i: (0, i, 0)),
            pl.BlockSpec((N_B_BLOCKS, tm, LANES), lambda i: (0, i, 0)),
            pl.BlockSpec((tm, 2 * d), lambda i: (i, 0)),
            pl.BlockSpec((tm, d), lambda i: (i, 0)),
            pl.BlockSpec((1, N_MOD, d), lambda i: (i // tiles_per_seq, 0, 0)),
            pl.BlockSpec((1, d), const),
            pl.BlockSpec(woa.shape, const),
            pl.BlockSpec(wob.shape, const),
            pl.BlockSpec(wout.shape, const),
        ],
        out_specs=pl.BlockSpec((tm, d), lambda i: (i, 0)),
        out_shape=jax.ShapeDtypeStruct((tokens, d), F32),
        compiler_params=_params("parallel"),
        name="mix_out",
    )(oa, ob, gg, x2, mod, g, woa, wob, wout)


def _ffn_kernel(x_ref, mod_ref, gpre_ref, gpost_ref, wgu_ref, wd_ref, o_ref, act_scr):
    d_ff = wd_ref.shape[0]
    x = x_ref[...]
    h = (_rms(x, gpre_ref[...]) * (1.0 + mod_ref[0, 4:5, :]) + mod_ref[0, 3:4, :]).astype(BF16)
    c0 = 0
    while c0 < d_ff:
        c1 = min(c0 + 512, d_ff)
        gate = _dot(h, wgu_ref[:, c0:c1])
        up = _dot(h, wgu_ref[:, d_ff + c0:d_ff + c1])
        act_scr[:, c0:c1] = (gate * jax.nn.sigmoid(gate) * up).astype(BF16)
        c0 = c1
    y = _dot(act_scr[...], wd_ref[...])
    o_ref[...] = x + mod_ref[0, 5:6, :] * _rms(y, gpost_ref[...])


def _ffn(x2, mod, gpre, gpost, wgu, wd, seq):
    tokens, d = x2.shape
    d_ff = wd.shape[0]
    tm = 512
    tiles_per_seq = seq // tm
    const = lambda i: (0, 0)
    resident = dict(pipeline_mode=pl.Buffered(1))
    return pl.pallas_call(
        _ffn_kernel,
        grid=(tokens // tm,),
        in_specs=[
            pl.BlockSpec((tm, d), lambda i: (i, 0)),
            pl.BlockSpec((1, N_MOD, d), lambda i: (i // tiles_per_seq, 0, 0)),
            pl.BlockSpec((1, d), const),
            pl.BlockSpec((1, d), const),
            pl.BlockSpec(wgu.shape, const, **resident),
            pl.BlockSpec(wd.shape, const, **resident),
        ],
        out_specs=pl.BlockSpec((tm, d), lambda i: (i, 0)),
        out_shape=jax.ShapeDtypeStruct((tokens, d), F32),
        scratch_shapes=[pltpu.VMEM((tm, d_ff), BF16)],
        compiler_params=_params("parallel"),
        name="ffn",
    )(x2, mod, gpre, gpost, wgu, wd)


def _alibi_slopes():
    i = np.arange(1, N_ALIBI + 1, dtype=np.float32)
    s = (2.0 ** (-8.0 * i / N_ALIBI)).astype(np.float32)
    return s[:A_Q_HEADS], s[A_Q_HEADS:]


def _split_w_in(w_in_l, d):
    widths = (A_Q_W, A_KV_W, A_KV_W, B_W, B_W, B_W, d, d)
    offs = np.concatenate([[0], np.cumsum(widths)])
    a_q, a_k, a_v, b_q, b_k, b_v, g_a, g_b = (w_in_l[:, offs[n]:offs[n + 1]] for n in range(8))
    a_q = jnp.concatenate([a_q[:, h * HEAD_DIM:(h + 1) * HEAD_DIM] for h in A_HEAD_ORDER], axis=1)
    w_std = jnp.concatenate([g_a, g_b, b_k, a_k], axis=1).astype(BF16)
    w_t = jnp.concatenate([a_q, b_q, b_v, a_v], axis=1).T.astype(BF16)
    return w_std, w_t


def kernel(x, c, ada_w, ada_b, norm_pre_mix, norm_post_mix, w_in, attn_sinks, w_o_a, w_o_b, w_out,
           norm_pre_ffn, norm_post_ffn, w_gate_up, w_down):
    batch, seq, d = x.shape
    depth = ada_w.shape[0]
    assert seq % 512 == 0 and d % LANES == 0

    slopes_a, slopes_b = _alibi_slopes()
    a_order = np.asarray(A_HEAD_ORDER).reshape(N_A_BLOCKS, HEADS_PER_LANE_BLOCK)
    slope_a_rows = jnp.asarray(np.repeat(slopes_a[a_order], WINDOW, axis=1)[:, None, :])
    slope_b_rows = jnp.asarray(np.broadcast_to(
        slopes_b.reshape(N_B_BLOCKS, HEADS_PER_LANE_BLOCK, 1, 1),
        (N_B_BLOCKS, HEADS_PER_LANE_BLOCK, 1, MOBA_BLOCK)))

    mod_all = _ada(c, ada_w, ada_b).reshape(depth, batch, N_MOD, d)
    x2 = x.reshape(batch * seq, d)
    for l in range(depth):
        mod = mod_all[l]
        w_std, w_t = _split_w_in(w_in[l], d)
        sink_rows = jnp.repeat(attn_sinks[l][a_order], WINDOW, axis=1)[:, None, :]
        woa = jnp.concatenate([w_o_a[l][h * HEAD_DIM:(h + 1) * HEAD_DIM] for h in A_HEAD_ORDER],
                              axis=0).astype(BF16)
        gg, kb, ka, qta, qtb, vtb, vta = _inproj(x2, mod, norm_pre_mix[l][None], w_std, w_t, seq)
        oa = _swa(qta, ka, vta, slope_a_rows, sink_rows, batch, seq)
        ob = _moba(qtb, kb, vtb, slope_b_rows, batch, seq)
        x2 = _mixout(oa, ob, gg, x2, mod, norm_post_mix[l][None], woa, w_o_b[l].astype(BF16),
                     w_out[l].astype(BF16), seq)
        x2 = _ffn(x2, mod, norm_pre_ffn[l][None], norm_post_ffn[l][None],
                  w_gate_up[l].astype(BF16), w_down[l].astype(BF16), seq)
    return x2.reshape(batch, seq, d)
```

```python
import functools

import numpy as np
import jax
import jax.numpy as jnp
from jax import lax
from jax.experimental import pallas as pl
from jax.experimental.pallas import tpu as pltpu

HEAD_DIM = 64
A_Q_HEADS = 8
A_KV_HEADS = 2
A_GROUP = A_Q_HEADS // A_KV_HEADS
WINDOW = 128
B_HEADS = 8
MOBA_BLOCK = 256
MOBA_TOPK = 3
N_ALIBI = A_Q_HEADS + B_HEADS
N_MOD = 6
EPS = 1e-6

LANES = 128
HEADS_PER_LANE_BLOCK = LANES // HEAD_DIM
A_Q_W = A_Q_HEADS * HEAD_DIM
A_KV_W = A_KV_HEADS * HEAD_DIM
B_W = B_HEADS * HEAD_DIM
N_A_BLOCKS = A_Q_W // LANES
N_B_BLOCKS = B_W // LANES
VMEM_LIMIT_BYTES = 56 * 1024 * 1024

A_HEAD_ORDER = tuple(h for f in range(N_A_BLOCKS) for h in (f, A_GROUP + f))

F32 = jnp.float32
BF16 = jnp.bfloat16
NEG_INF = float("-inf")
LOG2E = 1.4426950408889634


def _dot(a, b):
    return jnp.dot(a, b, preferred_element_type=F32)


def _dot_nt(a, b):
    return lax.dot_general(a, b, (((1,), (1,)), ((), ())), preferred_element_type=F32)


def _rms(x, g):
    return x * lax.rsqrt(jnp.mean(x * x, axis=-1, keepdims=True) + EPS) * g


def _layer_block(arr, l):
    tail = arr.shape[1:]
    return pl.BlockSpec((None,) + tail, lambda *_: (l,) + (0,) * len(tail))


def _params(*semantics):
    return pltpu.CompilerParams(dimension_semantics=semantics, vmem_limit_bytes=VMEM_LIMIT_BYTES)


def _ada_kernel(c_ref, w_ref, b_ref, o_ref):
    c = c_ref[...]
    cond = c * jax.nn.sigmoid(c)
    w = w_ref[0]
    c_hi = cond.astype(BF16)
    c_lo = (cond - c_hi.astype(F32)).astype(BF16)
    w_hi = w.astype(BF16)
    w_lo = (w - w_hi.astype(F32)).astype(BF16)
    o_ref[0] = _dot(c_hi, w_hi) + _dot(c_hi, w_lo) + _dot(c_lo, w_hi) + b_ref[0]


def _ada(c, ada_w, ada_b):
    depth, d, n = ada_w.shape
    batch = c.shape[0]
    tn = 1024
    return pl.pallas_call(
        _ada_kernel,
        grid=(depth, n // tn),
        in_specs=[
            pl.BlockSpec((batch, d), lambda l, j: (0, 0)),
            pl.BlockSpec((1, d, tn), lambda l, j: (l, 0, j)),
            pl.BlockSpec((1, 1, tn), lambda l, j: (l, 0, j)),
        ],
        out_specs=pl.BlockSpec((1, batch, tn), lambda l, j: (l, 0, j)),
        out_shape=jax.ShapeDtypeStruct((depth, batch, n), F32),
        compiler_params=_params("parallel", "parallel"),
        name="ada_mod",
    )(c, ada_w, ada_b.reshape(depth, 1, n))


def _inproj_kernel(x_ref, mod_ref, g_ref, wstd_ref, wt_ref,
                   gg_ref, kb_ref, ka_ref, qta_ref, qtb_ref, vtb_ref, vta_ref, *, d_model):
    tm = x_ref.shape[0]
    x = x_ref[...]
    h = _rms(x, g_ref[...]) * (1.0 + mod_ref[0, 1:2, :]) + mod_ref[0, 0:1, :]
    hb = h.astype(BF16)

    n_gate = 2 * d_model
    for c0 in range(0, n_gate, 512):
        gg_ref[:, c0:c0 + 512] = _dot(hb, wstd_ref[:, c0:c0 + 512]).astype(BF16)
    kb = _dot(hb, wstd_ref[:, n_gate:n_gate + B_W])
    for f in range(N_B_BLOCKS):
        kb_ref[f] = kb[:, f * LANES:(f + 1) * LANES].astype(BF16)
    ka_ref[...] = _dot(hb, wstd_ref[:, n_gate + B_W:n_gate + B_W + A_KV_W]).astype(BF16)

    scale = HEAD_DIM ** -0.5 * LOG2E
    qa = _dot_nt(wt_ref[0:A_Q_W, :], hb) * scale
    for f in range(N_A_BLOCKS):
        for t in range(tm // WINDOW):
            qta_ref[f, t] = qa[f * LANES:(f + 1) * LANES, t * WINDOW:(t + 1) * WINDOW].astype(BF16)
    qb = _dot_nt(wt_ref[A_Q_W:A_Q_W + B_W, :], hb) * scale
    for f in range(N_B_BLOCKS):
        for t in range(tm // MOBA_BLOCK):
            qtb_ref[f, t] = qb[f * LANES:(f + 1) * LANES, t * MOBA_BLOCK:(t + 1) * MOBA_BLOCK].astype(BF16)
    vb = _dot_nt(wt_ref[A_Q_W + B_W:A_Q_W + 2 * B_W, :], hb)
    for f in range(N_B_BLOCKS):
        for t in range(tm // MOBA_BLOCK):
            vtb_ref[f, t] = vb[f * LANES:(f + 1) * LANES, t * MOBA_BLOCK:(t + 1) * MOBA_BLOCK].astype(BF16)
    va = _dot_nt(wt_ref[A_Q_W + 2 * B_W:A_Q_W + 2 * B_W + A_KV_W, :], hb)
    for t in range(tm // WINDOW):
        vta_ref[t] = va[:, t * WINDOW:(t + 1) * WINDOW].astype(BF16)


def _inproj(x2, mod, g, w_std, w_t, seq, l):
    tokens, d = x2.shape
    tm = 512
    tiles_per_seq = seq // tm
    out_shape = (
        jax.ShapeDtypeStruct((tokens, 2 * d), BF16),
        jax.ShapeDtypeStruct((N_B_BLOCKS, tokens, LANES), BF16),
        jax.ShapeDtypeStruct((tokens, LANES), BF16),
        jax.ShapeDtypeStruct((N_A_BLOCKS, tokens // WINDOW, LANES, WINDOW), BF16),
        jax.ShapeDtypeStruct((N_B_BLOCKS, tokens // MOBA_BLOCK, LANES, MOBA_BLOCK), BF16),
        jax.ShapeDtypeStruct((N_B_BLOCKS, tokens // MOBA_BLOCK, LANES, MOBA_BLOCK), BF16),
        jax.ShapeDtypeStruct((tokens // WINDOW, LANES, WINDOW), BF16),
    )
    out_specs = (
        pl.BlockSpec((tm, 2 * d), lambda i: (i, 0)),
        pl.BlockSpec((N_B_BLOCKS, tm, LANES), lambda i: (0, i, 0)),
        pl.BlockSpec((tm, LANES), lambda i: (i, 0)),
        pl.BlockSpec((N_A_BLOCKS, tm // WINDOW, LANES, WINDOW), lambda i: (0, i, 0, 0)),
        pl.BlockSpec((N_B_BLOCKS, tm // MOBA_BLOCK, LANES, MOBA_BLOCK), lambda i: (0, i, 0, 0)),
        pl.BlockSpec((N_B_BLOCKS, tm // MOBA_BLOCK, LANES, MOBA_BLOCK), lambda i: (0, i, 0, 0)),
        pl.BlockSpec((tm // WINDOW, LANES, WINDOW), lambda i: (i, 0, 0)),
    )
    return pl.pallas_call(
        functools.partial(_inproj_kernel, d_model=d),
        grid=(tokens // tm,),
        in_specs=[
            pl.BlockSpec((tm, d), lambda i: (i, 0)),
            pl.BlockSpec((None, 1, N_MOD, d), lambda i: (l, i // tiles_per_seq, 0, 0)),
            _layer_block(g, l),
            _layer_block(w_std, l),
            _layer_block(w_t, l),
        ],
        out_specs=out_specs,
        out_shape=out_shape,
        compiler_params=_params("parallel"),
        name="in_proj",
    )(x2, mod, g, w_std, w_t)


def _swa_kernel(qt_ref, k_ref, vt_ref, slope_ref, sink_ref, o_ref, bias_scr):
    n_blocks = qt_ref.shape[1]
    slope = slope_ref[0] * LOG2E
    sink = sink_ref[0] * LOG2E
    lo_rows = lax.broadcasted_iota(jnp.int32, (LANES, WINDOW), 0) < HEAD_DIM

    def bias(n_keys, offset):
        r = lax.broadcasted_iota(jnp.int32, (n_keys, 2 * WINDOW), 0)
        col = lax.broadcasted_iota(jnp.int32, (n_keys, 2 * WINDOW), 1)
        qpos = jnp.where(col >= WINDOW, col - WINDOW, col)
        dist = offset + qpos - r
        valid = (dist >= 0) & (dist < WINDOW)
        return jnp.where(valid, -slope * dist.astype(F32), NEG_INF)

    bias_scr[...] = bias(2 * WINDOW, WINDOW)
    bias_first = bias(WINDOW, 0)

    def scores(n):
        qt = qt_ref[0, n]
        zero = jnp.zeros_like(qt)
        qt2 = jnp.concatenate([jnp.where(lo_rows, qt, zero), jnp.where(lo_rows, zero, qt)], axis=1)
        if n == 0:
            return _dot(k_ref[0:WINDOW, :], qt2) + bias_first
        return _dot(k_ref[(n - 1) * WINDOW:(n + 1) * WINDOW, :], qt2) + bias_scr[...]

    def softmax(z):
        m = jnp.maximum(jnp.max(z, axis=0, keepdims=True), sink)
        p = jnp.exp2(z - m)
        denom = jnp.sum(p, axis=0, keepdims=True) + jnp.exp2(sink - m)
        return p.astype(BF16), denom

    def finish(n, p, denom):
        vt_win = vt_ref[0] if n == 0 else jnp.concatenate([vt_ref[n - 1], vt_ref[n]], axis=1)
        ot = _dot(vt_win, p) * (1.0 / denom)
        o_t = jnp.where(lo_rows, ot[:, :WINDOW], ot[:, WINDOW:])
        o_ref[0, n * WINDOW:(n + 1) * WINDOW, :] = o_t.T.astype(BF16)

    z = pd = None
    for s in range(n_blocks + 2):
        z_new = scores(s) if s < n_blocks else None
        pd_new = softmax(z) if 1 <= s <= n_blocks else None
        if s >= 2:
            finish(s - 2, *pd)
        z, pd = z_new, pd_new


def _swa(qta, ka, vta, slope_rows, sink_rows, batch, seq, l):
    tokens = ka.shape[0]
    nblk = seq // WINDOW
    return pl.pallas_call(
        _swa_kernel,
        grid=(batch, N_A_BLOCKS),
        in_specs=[
            pl.BlockSpec((1, nblk, LANES, WINDOW), lambda b, f: (f, b, 0, 0)),
            pl.BlockSpec((seq, LANES), lambda b, f: (b, 0)),
            pl.BlockSpec((nblk, LANES, WINDOW), lambda b, f: (b, 0, 0)),
            pl.BlockSpec((1, 1, 2 * WINDOW), lambda b, f: (f, 0, 0)),
            pl.BlockSpec((None, 1, 1, 2 * WINDOW), lambda b, f: (l, f, 0, 0)),
        ],
        out_specs=pl.BlockSpec((1, seq, LANES), lambda b, f: (f, b, 0)),
        out_shape=jax.ShapeDtypeStruct((N_A_BLOCKS, tokens, LANES), BF16),
        scratch_shapes=[pltpu.VMEM((2 * WINDOW, 2 * WINDOW), F32)],
        compiler_params=_params("parallel", "parallel"),
        name="swa_attn",
    )(qta, ka, vta, slope_rows, sink_rows)


def _moba_kernel(qt_ref, k_ref, vt_ref, slope_ref, o_ref, bias_scr, z_scr, p_scr, vt1_scr):
    nb = qt_ref.shape[1]
    blk = MOBA_BLOCK
    seq = nb * blk
    feat = lax.broadcasted_iota(jnp.int32, (LANES, blk), 0)
    head_rows = (feat < HEAD_DIM, feat >= HEAD_DIM)
    blk_row = lax.broadcasted_iota(jnp.int32, (nb, blk), 0)

    dist = (seq - blk
            + lax.broadcasted_iota(jnp.int32, (seq, blk), 1)
            - lax.broadcasted_iota(jnp.int32, (seq, blk), 0))
    for h in range(HEADS_PER_LANE_BLOCK):
        bias_scr[h] = jnp.where(dist >= 0, -(slope_ref[0, h] * LOG2E) * dist.astype(F32), NEG_INF)

    tok = lax.broadcasted_iota(jnp.int32, (nb, seq), 1)
    blk_start = lax.broadcasted_iota(jnp.int32, (nb, seq), 0) * blk
    avg = jnp.where((tok >= blk_start) & (tok < blk_start + blk), 1.0 / blk, 0.0)
    k_mean = _dot(avg.astype(BF16), k_ref[0])
    km_hi = k_mean.astype(BF16)
    km_lo = (k_mean - km_hi.astype(F32)).astype(BF16)

    for j in range(nb):
        vt = vt_ref[0, j]
        for h in range(HEADS_PER_LANE_BLOCK):
            vt1_scr[h, j] = jnp.where(head_rows[h], vt, jnp.ones_like(vt))

    tasks = [(i, h) for i in range(nb) for h in range(HEADS_PER_LANE_BLOCK)]
    state = [dict() for _ in tasks]

    def scores_begin(t):
        i, h = tasks[t]
        qt = qt_ref[0, i]
        qt_h = jnp.where(head_rows[h], qt, jnp.zeros_like(qt))
        st = state[t]
        st["qt"] = qt_h
        st["cm"] = []
        if i > MOBA_TOPK:
            gate = _dot(km_hi, qt_h) + _dot(km_lo, qt_h)
            rank = jnp.zeros((nb, blk), jnp.int32)
            for jp in range(i):
                g_jp = gate[jp:jp + 1, :]
                beats = (g_jp > gate) | ((g_jp == gate) & (jp < blk_row))
                rank = rank + jnp.where(beats, 1, 0)
            st["unsel"] = jnp.where(rank < MOBA_TOPK, 0.0, NEG_INF)

    def scores_tile(t, j):
        i, h = tasks[t]
        st = state[t]
        row0 = seq - blk * (i + 1) + blk * j
        z = _dot(k_ref[0, j * blk:(j + 1) * blk, :], st["qt"]) + bias_scr[h, row0:row0 + blk, :]
        z_scr[t % 2, j] = z
        cm = jnp.max(z, axis=0, keepdims=True)
        if "unsel" in st and j < i:
            cm = cm + st["unsel"][j:j + 1, :]
        st["cm"].append(cm)

    def scores_end(t):
        i, h = tasks[t]
        st = state[t]
        m = functools.reduce(jnp.maximum, st["cm"])
        st["shift"] = [m - st["unsel"][j:j + 1, :] if ("unsel" in st and j < i) else m
                       for j in range(i + 1)]

    def probs_tile(t, j):
        p = jnp.exp2(z_scr[t % 2, j] - state[t]["shift"][j])
        p_scr[t % 2, j] = p.astype(BF16)

    def values_tile(t, j):
        i, h = tasks[t]
        st = state[t]
        part = _dot(vt1_scr[h, j], p_scr[t % 2, j])
        st["acc"] = part if j == 0 else st["acc"] + part

    def values_end(t):
        i, h = tasks[t]
        if h != HEADS_PER_LANE_BLOCK - 1:
            return
        acc0, acc1 = state[t - 1]["acc"], state[t]["acc"]
        denom0 = acc0[HEAD_DIM:HEAD_DIM + 1, :]
        denom1 = acc1[0:1, :]
        o_t = jnp.where(head_rows[0], acc0 * (1.0 / denom0), acc1 * (1.0 / denom1))
        o_ref[0, i * blk:(i + 1) * blk, :] = o_t.T.astype(BF16)
        state[t - 1].clear()
        state[t].clear()

    n_tasks = len(tasks)
    for step in range(n_tasks + 2):
        t_s, t_p, t_v = step, step - 1, step - 2
        live = [t for t in (t_s, t_p, t_v) if 0 <= t < n_tasks]
        if 0 <= t_s < n_tasks:
            scores_begin(t_s)
        for j in range(max(tasks[t][0] + 1 for t in live)):
            if 0 <= t_s < n_tasks and j <= tasks[t_s][0]:
                scores_tile(t_s, j)
            if 0 <= t_p < n_tasks and j <= tasks[t_p][0]:
                probs_tile(t_p, j)
            if 0 <= t_v < n_tasks and j <= tasks[t_v][0]:
                values_tile(t_v, j)
        if 0 <= t_s < n_tasks:
            scores_end(t_s)
        if 0 <= t_v < n_tasks:
            values_end(t_v)


def _moba(qtb, kb, vtb, slope_rows, batch, seq):
    tokens = kb.shape[1]
    nb = seq // MOBA_BLOCK
    return pl.pallas_call(
        _moba_kernel,
        grid=(batch, N_B_BLOCKS),
        in_specs=[
            pl.BlockSpec((1, nb, LANES, MOBA_BLOCK), lambda b, f: (f, b, 0, 0)),
            pl.BlockSpec((1, seq, LANES), lambda b, f: (f, b, 0)),
            pl.BlockSpec((1, nb, LANES, MOBA_BLOCK), lambda b, f: (f, b, 0, 0)),
            pl.BlockSpec((1, HEADS_PER_LANE_BLOCK, 1, MOBA_BLOCK), lambda b, f: (f, 0, 0, 0)),
        ],
        out_specs=pl.BlockSpec((1, seq, LANES), lambda b, f: (f, b, 0)),
        out_shape=jax.ShapeDtypeStruct((N_B_BLOCKS, tokens, LANES), BF16),
        scratch_shapes=[
            pltpu.VMEM((HEADS_PER_LANE_BLOCK, seq, MOBA_BLOCK), F32),
            pltpu.VMEM((2, nb, MOBA_BLOCK, MOBA_BLOCK), F32),
            pltpu.VMEM((2, nb, MOBA_BLOCK, MOBA_BLOCK), BF16),
            pltpu.VMEM((HEADS_PER_LANE_BLOCK, nb, LANES, MOBA_BLOCK), BF16),
        ],
        compiler_params=_params("parallel", "parallel"),
        name="moba_attn",
    )(qtb, kb, vtb, slope_rows)


def _mixout_kernel(oa_ref, ob_ref, gg_ref, x_ref, mod_ref, g_ref, woa_ref, wob_ref, wout_ref, o_ref):
    d = x_ref.shape[1]
    oa = jnp.concatenate([oa_ref[f] for f in range(N_A_BLOCKS)], axis=1)
    ob = jnp.concatenate([ob_ref[f] for f in range(N_B_BLOCKS)], axis=1)
    gg = gg_ref[...].astype(F32)
    merged = (jax.nn.sigmoid(gg[:, :d]) * _dot(oa, woa_ref[...])
              + jax.nn.sigmoid(gg[:, d:]) * _dot(ob, wob_ref[...]))
    y = _dot(merged.astype(BF16), wout_ref[...])
    o_ref[...] = x_ref[...] + mod_ref[0, 2:3, :] * _rms(y, g_ref[...])


def _mixout(oa, ob, gg, x2, mod, g, woa, wob, wout, seq, l):
    tokens, d = x2.shape
    tm = 512
    tiles_per_seq = seq // tm
    return pl.pallas_call(
        _mixout_kernel,
        grid=(tokens // tm,),
        in_specs=[
            pl.BlockSpec((N_A_BLOCKS, tm, LANES), lambda i: (0, i, 0)),
            pl.BlockSpec((N_B_BLOCKS, tm, LANES), lambda i: (0, i, 0)),
            pl.BlockSpec((tm, 2 * d), lambda i: (i, 0)),
            pl.BlockSpec((tm, d), lambda i: (i, 0)),
            pl.BlockSpec((None, 1, N_MOD, d), lambda i: (l, i // tiles_per_seq, 0, 0)),
            _layer_block(g, l),
            _layer_block(woa, l),
            _layer_block(wob, l),
            _layer_block(wout, l),
        ],
        out_specs=pl.BlockSpec((tm, d), lambda i: (i, 0)),
        out_shape=jax.ShapeDtypeStruct((tokens, d), F32),
        compiler_params=_params("parallel"),
        name="mix_out",
    )(oa, ob, gg, x2, mod, g, woa, wob, wout)


def _ffn_kernel(x_ref, mod_ref, gpre_ref, gpost_ref, wgu_ref, wd_ref, o_ref, act_scr):
    d_ff = wd_ref.shape[0]
    x = x_ref[...]
    h = (_rms(x, gpre_ref[...]) * (1.0 + mod_ref[0, 4:5, :]) + mod_ref[0, 3:4, :]).astype(BF16)
    c0 = 0
    while c0 < d_ff:
        c1 = min(c0 + 512, d_ff)
        gate = _dot(h, wgu_ref[:, c0:c1])
        up = _dot(h, wgu_ref[:, d_ff + c0:d_ff + c1])
        act_scr[:, c0:c1] = (gate * jax.nn.sigmoid(gate) * up).astype(BF16)
        c0 = c1
    y = _dot(act_scr[...], wd_ref[...])
    o_ref[...] = x + mod_ref[0, 5:6, :] * _rms(y, gpost_ref[...])


def _ffn(x2, mod, gpre, gpost, wgu, wd, seq, l):
    tokens, d = x2.shape
    d_ff = wd.shape[1]
    tm = 512
    tiles_per_seq = seq // tm

    def resident(arr):
        spec = _layer_block(arr, l)
        return pl.BlockSpec(spec.block_shape, spec.index_map, pipeline_mode=pl.Buffered(1))

    return pl.pallas_call(
        _ffn_kernel,
        grid=(tokens // tm,),
        in_specs=[
            pl.BlockSpec((tm, d), lambda i: (i, 0)),
            pl.BlockSpec((None, 1, N_MOD, d), lambda i: (l, i // tiles_per_seq, 0, 0)),
            _layer_block(gpre, l),
            _layer_block(gpost, l),
            resident(wgu),
            resident(wd),
        ],
        out_specs=pl.BlockSpec((tm, d), lambda i: (i, 0)),
        out_shape=jax.ShapeDtypeStruct((tokens, d), F32),
        scratch_shapes=[pltpu.VMEM((tm, d_ff), BF16)],
        compiler_params=_params("parallel"),
        name="ffn",
    )(x2, mod, gpre, gpost, wgu, wd)


def _alibi_slopes():
    i = np.arange(1, N_ALIBI + 1, dtype=np.float32)
    s = (2.0 ** (-8.0 * i / N_ALIBI)).astype(np.float32)
    return s[:A_Q_HEADS], s[A_Q_HEADS:]


def _prep_w_in(w_in, d):
    depth = w_in.shape[0]
    offs = np.concatenate([[0], np.cumsum((A_Q_W, A_KV_W, A_KV_W, B_W, B_W, B_W, d, d))])
    a_q, a_k, a_v, b_q, b_k, b_v, g_a, g_b = (w_in[:, :, offs[n]:offs[n + 1]] for n in range(8))
    a_q = a_q.reshape(depth, d, A_KV_HEADS, A_GROUP, HEAD_DIM).swapaxes(2, 3).reshape(depth, d, A_Q_W)
    w_std = jnp.concatenate([g_a, g_b, b_k, a_k], axis=2).astype(BF16)
    w_t = jnp.concatenate([a_q, b_q, b_v, a_v], axis=2).astype(BF16).swapaxes(1, 2)
    return w_std, w_t


def kernel(x, c, ada_w, ada_b, norm_pre_mix, norm_post_mix, w_in, attn_sinks, w_o_a, w_o_b, w_out,
           norm_pre_ffn, norm_post_ffn, w_gate_up, w_down):
    batch, seq, d = x.shape
    depth = ada_w.shape[0]
    assert seq % 512 == 0 and d % LANES == 0

    slopes_a, slopes_b = _alibi_slopes()
    a_order = np.asarray(A_HEAD_ORDER).reshape(N_A_BLOCKS, HEADS_PER_LANE_BLOCK)
    slope_a_rows = jnp.asarray(np.repeat(slopes_a[a_order], WINDOW, axis=1)[:, None, :])
    slope_b_rows = jnp.asarray(np.broadcast_to(
        slopes_b.reshape(N_B_BLOCKS, HEADS_PER_LANE_BLOCK, 1, 1),
        (N_B_BLOCKS, HEADS_PER_LANE_BLOCK, 1, MOBA_BLOCK)))

    sink_rows = jnp.repeat(attn_sinks.reshape(depth, A_KV_HEADS, A_GROUP).swapaxes(1, 2),
                           WINDOW, axis=2).reshape(depth, N_A_BLOCKS, 1, 2 * WINDOW)

    w_std, w_t = _prep_w_in(w_in, d)
    woa = (w_o_a.reshape(depth, A_KV_HEADS, A_GROUP, HEAD_DIM, d).swapaxes(1, 2)
           .reshape(depth, A_Q_W, d).astype(BF16))
    wob, wout = w_o_b.astype(BF16), w_out.astype(BF16)
    wgu, wd = w_gate_up.astype(BF16), w_down.astype(BF16)
    g_pre_mix, g_post_mix, g_pre_ffn, g_post_ffn = (
        g.reshape(depth, 1, d) for g in (norm_pre_mix, norm_post_mix, norm_pre_ffn, norm_post_ffn))

    mod = _ada(c, ada_w, ada_b).reshape(depth, batch, N_MOD, d)
    x2 = x.reshape(batch * seq, d)
    for l in range(depth):
        gg, kb, ka, qta, qtb, vtb, vta = _inproj(x2, mod, g_pre_mix, w_std, w_t, seq, l)
        oa = _swa(qta, ka, vta, slope_a_rows, sink_rows, batch, seq, l)
        ob = _moba(qtb, kb, vtb, slope_b_rows, batch, seq)
        x2 = _mixout(oa, ob, gg, x2, mod, g_post_mix, woa, wob, wout, seq, l)
        x2 = _ffn(x2, mod, g_pre_ffn, g_post_ffn, wgu, wd, seq, l)
    return x2.reshape(batch, seq, d)
```

```python
import functools

import numpy as np
import jax
import jax.numpy as jnp
from jax import lax
from jax.experimental import pallas as pl
from jax.experimental.pallas import tpu as pltpu

HEAD_DIM = 64
A_Q_HEADS = 8
A_KV_HEADS = 2
A_GROUP = A_Q_HEADS // A_KV_HEADS
WINDOW = 128
B_HEADS = 8
MOBA_BLOCK = 256
MOBA_TOPK = 3
N_ALIBI = A_Q_HEADS + B_HEADS
N_MOD = 6
EPS = 1e-6

LANES = 128
HEADS_PER_LANE_BLOCK = LANES // HEAD_DIM
A_Q_W = A_Q_HEADS * HEAD_DIM
A_KV_W = A_KV_HEADS * HEAD_DIM
B_W = B_HEADS * HEAD_DIM
N_A_BLOCKS = A_Q_W // LANES
N_B_BLOCKS = B_W // LANES
VMEM_LIMIT_BYTES = 56 * 1024 * 1024

A_HEAD_ORDER = tuple(h for f in range(N_A_BLOCKS) for h in (f, A_GROUP + f))

F32 = jnp.float32
BF16 = jnp.bfloat16
NEG_INF = float("-inf")
LOG2E = 1.4426950408889634


def _dot(a, b):
    return jnp.dot(a, b, preferred_element_type=F32)


def _dot_nt(a, b):
    return lax.dot_general(a, b, (((1,), (1,)), ((), ())), preferred_element_type=F32)


def _rms(x, g):
    return x * lax.rsqrt(jnp.mean(x * x, axis=-1, keepdims=True) + EPS) * g


def _layer_block(arr, l):
    tail = arr.shape[1:]
    return pl.BlockSpec((None,) + tail, lambda *_: (l,) + (0,) * len(tail))


def _params(*semantics):
    return pltpu.CompilerParams(dimension_semantics=semantics, vmem_limit_bytes=VMEM_LIMIT_BYTES)


def _ada_kernel(c_ref, w_ref, b_ref, o_ref):
    c = c_ref[...]
    cond = c * jax.nn.sigmoid(c)
    w = w_ref[0]
    c_hi = cond.astype(BF16)
    c_lo = (cond - c_hi.astype(F32)).astype(BF16)
    w_hi = w.astype(BF16)
    w_lo = (w - w_hi.astype(F32)).astype(BF16)
    o_ref[0] = _dot(c_hi, w_hi) + _dot(c_hi, w_lo) + _dot(c_lo, w_hi) + b_ref[0]


def _ada(c, ada_w, ada_b):
    depth, d, n = ada_w.shape
    batch = c.shape[0]
    tn = 1024
    return pl.pallas_call(
        _ada_kernel,
        grid=(depth, n // tn),
        in_specs=[
            pl.BlockSpec((batch, d), lambda l, j: (0, 0)),
            pl.BlockSpec((1, d, tn), lambda l, j: (l, 0, j)),
            pl.BlockSpec((1, 1, tn), lambda l, j: (l, 0, j)),
        ],
        out_specs=pl.BlockSpec((1, batch, tn), lambda l, j: (l, 0, j)),
        out_shape=jax.ShapeDtypeStruct((depth, batch, n), F32),
        compiler_params=_params("parallel", "parallel"),
        name="ada_mod",
    )(c, ada_w, ada_b.reshape(depth, 1, n))


def _inproj_kernel(x_ref, mod_ref, g_ref, wstd_ref, wt_ref,
                   gg_ref, kb_ref, ka_ref, qta_ref, qtb_ref, vtb_ref, vta_ref, *, d_model):
    tm = x_ref.shape[0]
    x = x_ref[...]
    h = _rms(x, g_ref[...]) * (1.0 + mod_ref[0, 1:2, :]) + mod_ref[0, 0:1, :]
    hb = h.astype(BF16)

    n_gate = 2 * d_model
    for c0 in range(0, n_gate, 512):
        gg_ref[:, c0:c0 + 512] = _dot(hb, wstd_ref[:, c0:c0 + 512]).astype(BF16)
    kb = _dot(hb, wstd_ref[:, n_gate:n_gate + B_W])
    for f in range(N_B_BLOCKS):
        kb_ref[f] = kb[:, f * LANES:(f + 1) * LANES].astype(BF16)
    ka_ref[...] = _dot(hb, wstd_ref[:, n_gate + B_W:n_gate + B_W + A_KV_W]).astype(BF16)

    scale = HEAD_DIM ** -0.5 * LOG2E
    qa = _dot_nt(wt_ref[0:A_Q_W, :], hb) * scale
    for f in range(N_A_BLOCKS):
        for t in range(tm // WINDOW):
            qta_ref[f, t] = qa[f * LANES:(f + 1) * LANES, t * WINDOW:(t + 1) * WINDOW].astype(BF16)
    qb = _dot_nt(wt_ref[A_Q_W:A_Q_W + B_W, :], hb) * scale
    for f in range(N_B_BLOCKS):
        for t in range(tm // MOBA_BLOCK):
            qtb_ref[f, t] = qb[f * LANES:(f + 1) * LANES, t * MOBA_BLOCK:(t + 1) * MOBA_BLOCK].astype(BF16)
    vb = _dot_nt(wt_ref[A_Q_W + B_W:A_Q_W + 2 * B_W, :], hb)
    for f in range(N_B_BLOCKS):
        for t in range(tm // MOBA_BLOCK):
            vtb_ref[f, t] = vb[f * LANES:(f + 1) * LANES, t * MOBA_BLOCK:(t + 1) * MOBA_BLOCK].astype(BF16)
    va = _dot_nt(wt_ref[A_Q_W + 2 * B_W:A_Q_W + 2 * B_W + A_KV_W, :], hb)
    for t in range(tm // WINDOW):
        vta_ref[t] = va[:, t * WINDOW:(t + 1) * WINDOW].astype(BF16)


def _inproj(x2, mod, g, w_std, w_t, seq, l):
    tokens, d = x2.shape
    tm = 512
    tiles_per_seq = seq // tm
    out_shape = (
        jax.ShapeDtypeStruct((tokens, 2 * d), BF16),
        jax.ShapeDtypeStruct((N_B_BLOCKS, tokens, LANES), BF16),
        jax.ShapeDtypeStruct((tokens, LANES), BF16),
        jax.ShapeDtypeStruct((N_A_BLOCKS, tokens // WINDOW, LANES, WINDOW), BF16),
        jax.ShapeDtypeStruct((N_B_BLOCKS, tokens // MOBA_BLOCK, LANES, MOBA_BLOCK), BF16),
        jax.ShapeDtypeStruct((N_B_BLOCKS, tokens // MOBA_BLOCK, LANES, MOBA_BLOCK), BF16),
        jax.ShapeDtypeStruct((tokens // WINDOW, LANES, WINDOW), BF16),
    )
    out_specs = (
        pl.BlockSpec((tm, 2 * d), lambda i: (i, 0)),
        pl.BlockSpec((N_B_BLOCKS, tm, LANES), lambda i: (0, i, 0)),
        pl.BlockSpec((tm, LANES), lambda i: (i, 0)),
        pl.BlockSpec((N_A_BLOCKS, tm // WINDOW, LANES, WINDOW), lambda i: (0, i, 0, 0)),
        pl.BlockSpec((N_B_BLOCKS, tm // MOBA_BLOCK, LANES, MOBA_BLOCK), lambda i: (0, i, 0, 0)),
        pl.BlockSpec((N_B_BLOCKS, tm // MOBA_BLOCK, LANES, MOBA_BLOCK), lambda i: (0, i, 0, 0)),
        pl.BlockSpec((tm // WINDOW, LANES, WINDOW), lambda i: (i, 0, 0)),
    )
    return pl.pallas_call(
        functools.partial(_inproj_kernel, d_model=d),
        grid=(tokens // tm,),
        in_specs=[
            pl.BlockSpec((tm, d), lambda i: (i, 0)),
            pl.BlockSpec((None, 1, N_MOD, d), lambda i: (l, i // tiles_per_seq, 0, 0)),
            _layer_block(g, l),
            _layer_block(w_std, l),
            _layer_block(w_t, l),
        ],
        out_specs=out_specs,
        out_shape=out_shape,
        compiler_params=_params("parallel"),
        name="in_proj",
    )(x2, mod, g, w_std, w_t)


def _swa_kernel(qt_ref, k_ref, vt_ref, slope_ref, sink_ref, o_ref, bias_scr):
    n_blocks = qt_ref.shape[1]
    slope = slope_ref[0] * LOG2E
    sink = sink_ref[0] * LOG2E
    lo_rows = lax.broadcasted_iota(jnp.int32, (LANES, WINDOW), 0) < HEAD_DIM

    def bias(n_keys, offset):
        r = lax.broadcasted_iota(jnp.int32, (n_keys, 2 * WINDOW), 0)
        col = lax.broadcasted_iota(jnp.int32, (n_keys, 2 * WINDOW), 1)
        qpos = jnp.where(col >= WINDOW, col - WINDOW, col)
        dist = offset + qpos - r
        valid = (dist >= 0) & (dist < WINDOW)
        return jnp.where(valid, -slope * dist.astype(F32), NEG_INF)

    bias_scr[...] = bias(2 * WINDOW, WINDOW)
    bias_first = bias(WINDOW, 0)

    def scores(n):
        qt = qt_ref[0, n]
        zero = jnp.zeros_like(qt)
        qt2 = jnp.concatenate([jnp.where(lo_rows, qt, zero), jnp.where(lo_rows, zero, qt)], axis=1)
        if n == 0:
            return _dot(k_ref[0:WINDOW, :], qt2) + bias_first
        return _dot(k_ref[(n - 1) * WINDOW:(n + 1) * WINDOW, :], qt2) + bias_scr[...]

    def softmax(z):
        m = jnp.maximum(jnp.max(z, axis=0, keepdims=True), sink)
        p = jnp.exp2(z - m)
        denom = jnp.sum(p, axis=0, keepdims=True) + jnp.exp2(sink - m)
        return p.astype(BF16), denom

    def finish(n, p, denom):
        vt_win = vt_ref[0] if n == 0 else jnp.concatenate([vt_ref[n - 1], vt_ref[n]], axis=1)
        ot = _dot(vt_win, p) * (1.0 / denom)
        o_t = jnp.where(lo_rows, ot[:, :WINDOW], ot[:, WINDOW:])
        o_ref[0, n * WINDOW:(n + 1) * WINDOW, :] = o_t.T.astype(BF16)

    z = pd = None
    for s in range(n_blocks + 2):
        z_new = scores(s) if s < n_blocks else None
        pd_new = softmax(z) if 1 <= s <= n_blocks else None
        if s >= 2:
            finish(s - 2, *pd)
        z, pd = z_new, pd_new


def _swa(qta, ka, vta, slope_rows, sink_rows, batch, seq, l):
    tokens = ka.shape[0]
    nblk = seq // WINDOW
    return pl.pallas_call(
        _swa_kernel,
        grid=(batch, N_A_BLOCKS),
        in_specs=[
            pl.BlockSpec((1, nblk, LANES, WINDOW), lambda b, f: (f, b, 0, 0)),
            pl.BlockSpec((seq, LANES), lambda b, f: (b, 0)),
            pl.BlockSpec((nblk, LANES, WINDOW), lambda b, f: (b, 0, 0)),
            pl.BlockSpec((1, 1, 2 * WINDOW), lambda b, f: (f, 0, 0)),
            pl.BlockSpec((None, 1, 1, 2 * WINDOW), lambda b, f: (l, f, 0, 0)),
        ],
        out_specs=pl.BlockSpec((1, seq, LANES), lambda b, f: (f, b, 0)),
        out_shape=jax.ShapeDtypeStruct((N_A_BLOCKS, tokens, LANES), BF16),
        scratch_shapes=[pltpu.VMEM((2 * WINDOW, 2 * WINDOW), F32)],
        compiler_params=_params("parallel", "parallel"),
        name="swa_attn",
    )(qta, ka, vta, slope_rows, sink_rows)


def _moba_kernel(qt_ref, k_ref, vt_ref, slope_ref, o_ref, bias_scr, z_scr, p_scr, vt1_scr):
    nb = qt_ref.shape[1]
    blk = MOBA_BLOCK
    seq = nb * blk
    feat = lax.broadcasted_iota(jnp.int32, (LANES, blk), 0)
    head_rows = (feat < HEAD_DIM, feat >= HEAD_DIM)
    blk_row = lax.broadcasted_iota(jnp.int32, (nb, blk), 0)

    dist = (seq - blk
            + lax.broadcasted_iota(jnp.int32, (seq, blk), 1)
            - lax.broadcasted_iota(jnp.int32, (seq, blk), 0))
    for h in range(HEADS_PER_LANE_BLOCK):
        bias_scr[h] = jnp.where(dist >= 0, -(slope_ref[0, h] * LOG2E) * dist.astype(F32), NEG_INF)

    tok = lax.broadcasted_iota(jnp.int32, (nb, seq), 1)
    blk_start = lax.broadcasted_iota(jnp.int32, (nb, seq), 0) * blk
    avg = jnp.where((tok >= blk_start) & (tok < blk_start + blk), 1.0 / blk, 0.0)
    k_mean = _dot(avg.astype(BF16), k_ref[0])
    km_hi = k_mean.astype(BF16)
    km_lo = (k_mean - km_hi.astype(F32)).astype(BF16)

    for j in range(nb):
        vt = vt_ref[0, j]
        for h in range(HEADS_PER_LANE_BLOCK):
            vt1_scr[h, j] = jnp.where(head_rows[h], vt, jnp.ones_like(vt))

    tasks = [(i, h) for i in range(nb) for h in range(HEADS_PER_LANE_BLOCK)]
    state = [dict() for _ in tasks]

    def scores_begin(t):
        i, h = tasks[t]
        qt = qt_ref[0, i]
        qt_h = jnp.where(head_rows[h], qt, jnp.zeros_like(qt))
        st = state[t]
        st["qt"] = qt_h
        st["cm"] = []
        if i > MOBA_TOPK:
            gate = _dot(km_hi, qt_h) + _dot(km_lo, qt_h)
            rank = jnp.zeros((nb, blk), jnp.int32)
            for jp in range(i):
                g_jp = gate[jp:jp + 1, :]
                beats = (g_jp > gate) | ((g_jp == gate) & (jp < blk_row))
                rank = rank + jnp.where(beats, 1, 0)
            st["unsel"] = jnp.where(rank < MOBA_TOPK, 0.0, NEG_INF)

    def scores_tile(t, j):
        i, h = tasks[t]
        st = state[t]
        row0 = seq - blk * (i + 1) + blk * j
        z = _dot(k_ref[0, j * blk:(j + 1) * blk, :], st["qt"]) + bias_scr[h, row0:row0 + blk, :]
        z_scr[t % 2, j] = z
        cm = jnp.max(z, axis=0, keepdims=True)
        if "unsel" in st and j < i:
            cm = cm + st["unsel"][j:j + 1, :]
        st["cm"].append(cm)

    def scores_end(t):
        i, h = tasks[t]
        st = state[t]
        m = functools.reduce(jnp.maximum, st["cm"])
        st["shift"] = [m - st["unsel"][j:j + 1, :] if ("unsel" in st and j < i) else m
                       for j in range(i + 1)]

    def probs_tile(t, j):
        p = jnp.exp2(z_scr[t % 2, j] - state[t]["shift"][j])
        p_scr[t % 2, j] = p.astype(BF16)

    def values_tile(t, j):
        i, h = tasks[t]
        st = state[t]
        part = _dot(vt1_scr[h, j], p_scr[t % 2, j])
        st["acc"] = part if j == 0 else st["acc"] + part

    def values_end(t):
        i, h = tasks[t]
        if h != HEADS_PER_LANE_BLOCK - 1:
            return
        acc0, acc1 = state[t - 1]["acc"], state[t]["acc"]
        denom0 = acc0[HEAD_DIM:HEAD_DIM + 1, :]
        denom1 = acc1[0:1, :]
        o_t = jnp.where(head_rows[0], acc0 * (1.0 / denom0), acc1 * (1.0 / denom1))
        o_ref[0, i * blk:(i + 1) * blk, :] = o_t.T.astype(BF16)
        state[t - 1].clear()
        state[t].clear()

    n_tasks = len(tasks)
    for step in range(n_tasks + 2):
        t_s, t_p, t_v = step, step - 1, step - 2
        live = [t for t in (t_s, t_p, t_v) if 0 <= t < n_tasks]
        if 0 <= t_s < n_tasks:
            scores_begin(t_s)
        for j in range(max(tasks[t][0] + 1 for t in live)):
            if 0 <= t_s < n_tasks and j <= tasks[t_s][0]:
                scores_tile(t_s, j)
            if 0 <= t_p < n_tasks and j <= tasks[t_p][0]:
                probs_tile(t_p, j)
            if 0 <= t_v < n_tasks and j <= tasks[t_v][0]:
                values_tile(t_v, j)
        if 0 <= t_s < n_tasks:
            scores_end(t_s)
        if 0 <= t_v < n_tasks:
            values_end(t_v)


def _moba(qtb, kb, vtb, slope_rows, batch, seq):
    tokens = kb.shape[1]
    nb = seq // MOBA_BLOCK
    return pl.pallas_call(
        _moba_kernel,
        grid=(batch, N_B_BLOCKS),
        in_specs=[
            pl.BlockSpec((1, nb, LANES, MOBA_BLOCK), lambda b, f: (f, b, 0, 0)),
            pl.BlockSpec((1, seq, LANES), lambda b, f: (f, b, 0)),
            pl.BlockSpec((1, nb, LANES, MOBA_BLOCK), lambda b, f: (f, b, 0, 0)),
            pl.BlockSpec((1, HEADS_PER_LANE_BLOCK, 1, MOBA_BLOCK), lambda b, f: (f, 0, 0, 0)),
        ],
        out_specs=pl.BlockSpec((1, seq, LANES), lambda b, f: (f, b, 0)),
        out_shape=jax.ShapeDtypeStruct((N_B_BLOCKS, tokens, LANES), BF16),
        scratch_shapes=[
            pltpu.VMEM((HEADS_PER_LANE_BLOCK, seq, MOBA_BLOCK), F32),
            pltpu.VMEM((2, nb, MOBA_BLOCK, MOBA_BLOCK), F32),
            pltpu.VMEM((2, nb, MOBA_BLOCK, MOBA_BLOCK), BF16),
            pltpu.VMEM((HEADS_PER_LANE_BLOCK, nb, LANES, MOBA_BLOCK), BF16),
        ],
        compiler_params=_params("parallel", "parallel"),
        name="moba_attn",
    )(qtb, kb, vtb, slope_rows)


FF_CHUNK = 512
OUT_CHUNK = 512


def _post_kernel(oa_ref, ob_ref, gg_ref, x_ref, mod_ref, gmix_ref, gpre_ref, gpost_ref,
                 woa_ref, wob_ref, wout_ref, wgu_ref, wd_ref, o_ref, act_scr, x1_scr):
    tm, d = x_ref.shape
    d_ff = wd_ref.shape[0]
    half = tm // 2
    ff_chunks = [(c0, min(c0 + FF_CHUNK, d_ff)) for c0 in range(0, d_ff, FF_CHUNK)]
    out_chunks = [(n0, n0 + OUT_CHUNK) for n0 in range(0, d, OUT_CHUNK)]

    def stages(r0):
        rows = slice(r0, r0 + half)
        st = {"y": []}

        def merge():
            oa = jnp.concatenate([oa_ref[f, rows, :] for f in range(N_A_BLOCKS)], axis=1)
            ob = jnp.concatenate([ob_ref[f, rows, :] for f in range(N_B_BLOCKS)], axis=1)
            gg = gg_ref[rows, :].astype(F32)
            st["merged"] = (jax.nn.sigmoid(gg[:, :d]) * _dot(oa, woa_ref[...])
                            + jax.nn.sigmoid(gg[:, d:]) * _dot(ob, wob_ref[...])).astype(BF16)

        def mix_residual():
            y = _dot(st.pop("merged"), wout_ref[...])
            x1_scr[rows, :] = x_ref[rows, :] + mod_ref[0, 2:3, :] * _rms(y, gmix_ref[...])

        def ffn_in():
            st["h"] = (_rms(x1_scr[rows, :], gpre_ref[...]) * (1.0 + mod_ref[0, 4:5, :])
                       + mod_ref[0, 3:4, :]).astype(BF16)

        def gate_up(c0, c1):
            gate = _dot(st["h"], wgu_ref[:, c0:c1])
            up = _dot(st["h"], wgu_ref[:, d_ff + c0:d_ff + c1])
            act_scr[rows, c0:c1] = (gate * jax.nn.sigmoid(gate) * up).astype(BF16)

        def down(n0, n1):
            st["y"].append(_dot(act_scr[rows, :], wd_ref[:, n0:n1]))

        def ffn_residual():
            y = jnp.concatenate(st["y"], axis=1)
            o_ref[rows, :] = x1_scr[rows, :] + mod_ref[0, 5:6, :] * _rms(y, gpost_ref[...])

        return ([merge, mix_residual, ffn_in]
                + [functools.partial(gate_up, c0, c1) for c0, c1 in ff_chunks]
                + [functools.partial(down, n0, n1) for n0, n1 in out_chunks]
                + [ffn_residual])

    first, second = stages(0), stages(half)
    lag = 3
    for n in range(len(first) + lag):
        if n < len(first):
            first[n]()
        if n >= lag:
            second[n - lag]()


def _post(oa, ob, gg, x2, mod, gmix, gpre, gpost, woa, wob, wout, wgu, wd, seq, l):
    tokens, d = x2.shape
    d_ff = wd.shape[1]
    tm = 512
    tiles_per_seq = seq // tm

    def resident(arr):
        spec = _layer_block(arr, l)
        return pl.BlockSpec(spec.block_shape, spec.index_map, pipeline_mode=pl.Buffered(1))

    return pl.pallas_call(
        _post_kernel,
        grid=(tokens // tm,),
        in_specs=[
            pl.BlockSpec((N_A_BLOCKS, tm, LANES), lambda i: (0, i, 0)),
            pl.BlockSpec((N_B_BLOCKS, tm, LANES), lambda i: (0, i, 0)),
            pl.BlockSpec((tm, 2 * d), lambda i: (i, 0)),
            pl.BlockSpec((tm, d), lambda i: (i, 0)),
            pl.BlockSpec((None, 1, N_MOD, d), lambda i: (l, i // tiles_per_seq, 0, 0)),
            _layer_block(gmix, l),
            _layer_block(gpre, l),
            _layer_block(gpost, l),
            resident(woa),
            resident(wob),
            resident(wout),
            resident(wgu),
            resident(wd),
        ],
        out_specs=pl.BlockSpec((tm, d), lambda i: (i, 0)),
        out_shape=jax.ShapeDtypeStruct((tokens, d), F32),
        scratch_shapes=[pltpu.VMEM((tm, d_ff), BF16), pltpu.VMEM((tm, d), F32)],
        compiler_params=_params("parallel"),
        name="mix_ffn",
    )(oa, ob, gg, x2, mod, gmix, gpre, gpost, woa, wob, wout, wgu, wd)


def _alibi_slopes():
    i = np.arange(1, N_ALIBI + 1, dtype=np.float32)
    s = (2.0 ** (-8.0 * i / N_ALIBI)).astype(np.float32)
    return s[:A_Q_HEADS], s[A_Q_HEADS:]


def _prep_w_in(w_in, d):
    depth = w_in.shape[0]
    offs = np.concatenate([[0], np.cumsum((A_Q_W, A_KV_W, A_KV_W, B_W, B_W, B_W, d, d))])
    a_q, a_k, a_v, b_q, b_k, b_v, g_a, g_b = (w_in[:, :, offs[n]:offs[n + 1]] for n in range(8))
    a_q = a_q.reshape(depth, d, A_KV_HEADS, A_GROUP, HEAD_DIM).swapaxes(2, 3).reshape(depth, d, A_Q_W)
    w_std = jnp.concatenate([g_a, g_b, b_k, a_k], axis=2).astype(BF16)
    w_t = jnp.concatenate([a_q, b_q, b_v, a_v], axis=2).astype(BF16).swapaxes(1, 2)
    return w_std, w_t


def kernel(x, c, ada_w, ada_b, norm_pre_mix, norm_post_mix, w_in, attn_sinks, w_o_a, w_o_b, w_out,
           norm_pre_ffn, norm_post_ffn, w_gate_up, w_down):
    batch, seq, d = x.shape
    depth = ada_w.shape[0]
    assert seq % 512 == 0 and d % LANES == 0

    slopes_a, slopes_b = _alibi_slopes()
    a_order = np.asarray(A_HEAD_ORDER).reshape(N_A_BLOCKS, HEADS_PER_LANE_BLOCK)
    slope_a_rows = jnp.asarray(np.repeat(slopes_a[a_order], WINDOW, axis=1)[:, None, :])
    slope_b_rows = jnp.asarray(np.broadcast_to(
        slopes_b.reshape(N_B_BLOCKS, HEADS_PER_LANE_BLOCK, 1, 1),
        (N_B_BLOCKS, HEADS_PER_LANE_BLOCK, 1, MOBA_BLOCK)))

    sink_rows = jnp.repeat(attn_sinks.reshape(depth, A_KV_HEADS, A_GROUP).swapaxes(1, 2),
                           WINDOW, axis=2).reshape(depth, N_A_BLOCKS, 1, 2 * WINDOW)

    w_std, w_t = _prep_w_in(w_in, d)
    woa = (w_o_a.reshape(depth, A_KV_HEADS, A_GROUP, HEAD_DIM, d).swapaxes(1, 2)
           .reshape(depth, A_Q_W, d).astype(BF16))
    wob, wout = w_o_b.astype(BF16), w_out.astype(BF16)
    wgu, wd = w_gate_up.astype(BF16), w_down.astype(BF16)
    g_pre_mix, g_post_mix, g_pre_ffn, g_post_ffn = (
        g.reshape(depth, 1, d) for g in (norm_pre_mix, norm_post_mix, norm_pre_ffn, norm_post_ffn))

    mod = _ada(c, ada_w, ada_b).reshape(depth, batch, N_MOD, d)
    x2 = x.reshape(batch * seq, d)
    for l in range(depth):
        gg, kb, ka, qta, qtb, vtb, vta = _inproj(x2, mod, g_pre_mix, w_std, w_t, seq, l)
        oa = _swa(qta, ka, vta, slope_a_rows, sink_rows, batch, seq, l)
        ob = _moba(qtb, kb, vtb, slope_b_rows, batch, seq)
        x2 = _post(oa, ob, gg, x2, mod, g_post_mix, g_pre_ffn, g_post_ffn, woa, wob, wout, wgu, wd, seq, l)
    return x2.reshape(batch, seq, d)
```

```python
import functools

import numpy as np
import jax
import jax.numpy as jnp
from jax import lax
from jax.experimental import pallas as pl
from jax.experimental.pallas import tpu as pltpu

HEAD_DIM = 64
A_Q_HEADS = 8
A_KV_HEADS = 2
A_GROUP = A_Q_HEADS // A_KV_HEADS
WINDOW = 128
B_HEADS = 8
MOBA_BLOCK = 256
MOBA_TOPK = 3
N_ALIBI = A_Q_HEADS + B_HEADS
N_MOD = 6
EPS = 1e-6

LANES = 128
HEADS_PER_LANE_BLOCK = LANES // HEAD_DIM
A_Q_W = A_Q_HEADS * HEAD_DIM
A_KV_W = A_KV_HEADS * HEAD_DIM
B_W = B_HEADS * HEAD_DIM
N_A_BLOCKS = A_Q_W // LANES
N_B_BLOCKS = B_W // LANES
VMEM_LIMIT_BYTES = 56 * 1024 * 1024

A_HEAD_ORDER = tuple(h for f in range(N_A_BLOCKS) for h in (f, A_GROUP + f))

F32 = jnp.float32
BF16 = jnp.bfloat16
NEG_INF = float("-inf")
LOG2E = 1.4426950408889634


def _dot(a, b):
    return jnp.dot(a, b, preferred_element_type=F32)


def _dot_nt(a, b):
    return lax.dot_general(a, b, (((1,), (1,)), ((), ())), preferred_element_type=F32)


def _rms(x, g):
    return x * lax.rsqrt(jnp.mean(x * x, axis=-1, keepdims=True) + EPS) * g


def _layer_block(arr, l):
    tail = arr.shape[1:]
    return pl.BlockSpec((None,) + tail, lambda *_: (l,) + (0,) * len(tail))


def _params(*semantics):
    return pltpu.CompilerParams(dimension_semantics=semantics, vmem_limit_bytes=VMEM_LIMIT_BYTES)


def _ada_kernel(c_ref, w_ref, b_ref, o_ref):
    c = c_ref[...]
    cond = c * jax.nn.sigmoid(c)
    w = w_ref[0]
    c_hi = cond.astype(BF16)
    c_lo = (cond - c_hi.astype(F32)).astype(BF16)
    w_hi = w.astype(BF16)
    w_lo = (w - w_hi.astype(F32)).astype(BF16)
    o_ref[0] = _dot(c_hi, w_hi) + _dot(c_hi, w_lo) + _dot(c_lo, w_hi) + b_ref[0]


def _ada(c, ada_w, ada_b):
    depth, d, n = ada_w.shape
    batch = c.shape[0]
    tn = 1024
    return pl.pallas_call(
        _ada_kernel,
        grid=(depth, n // tn),
        in_specs=[
            pl.BlockSpec((batch, d), lambda l, j: (0, 0)),
            pl.BlockSpec((1, d, tn), lambda l, j: (l, 0, j)),
            pl.BlockSpec((1, 1, tn), lambda l, j: (l, 0, j)),
        ],
        out_specs=pl.BlockSpec((1, batch, tn), lambda l, j: (l, 0, j)),
        out_shape=jax.ShapeDtypeStruct((depth, batch, n), F32),
        compiler_params=_params("parallel", "parallel"),
        name="ada_mod",
    )(c, ada_w, ada_b.reshape(depth, 1, n))


def _w_in_offsets(d):
    offs = np.concatenate([[0], np.cumsum((A_Q_W, A_KV_W, A_KV_W, B_W, B_W, B_W, d, d))])
    return {name: int(o) for name, o in zip(("a_q", "a_k", "a_v", "b_q", "b_k", "b_v", "g_a", "g_b"), offs)}


WT_ROWS = A_Q_W + 2 * B_W + A_KV_W


def _inproj_kernel(x_ref, mod_ref, g_ref, win_ref,
                   gg_ref, kb_ref, ka_ref, qta_ref, qtb_ref, vtb_ref, vta_ref, wt_scr):
    tm, d = x_ref.shape
    half = MOBA_BLOCK
    col = _w_in_offsets(d)
    row_bq, row_bv, row_av = A_Q_W, A_Q_W + B_W, A_Q_W + 2 * B_W

    @pl.when(pl.program_id(0) == 0)
    def _():
        def transposed(c0):
            return win_ref[:, c0:c0 + LANES].astype(F32).T.astype(BF16)

        for pair in range(N_A_BLOCKS):
            w_pair = transposed(col["a_q"] + pair * LANES)
            for part in range(HEADS_PER_LANE_BLOCK):
                slot = A_HEAD_ORDER.index(pair * HEADS_PER_LANE_BLOCK + part)
                wt_scr[slot * HEAD_DIM:(slot + 1) * HEAD_DIM, :] = w_pair[part * HEAD_DIM:(part + 1) * HEAD_DIM, :]
        for f in range(N_B_BLOCKS):
            wt_scr[row_bq + f * LANES:row_bq + (f + 1) * LANES, :] = transposed(col["b_q"] + f * LANES)
            wt_scr[row_bv + f * LANES:row_bv + (f + 1) * LANES, :] = transposed(col["b_v"] + f * LANES)
        wt_scr[row_av:row_av + A_KV_W, :] = transposed(col["a_v"])

    scale = HEAD_DIM ** -0.5 * LOG2E

    def stages(r0):
        rows = slice(r0, r0 + half)
        t_moba = r0 // MOBA_BLOCK
        t_win = r0 // WINDOW
        st = {}

        def pre():
            h = _rms(x_ref[rows, :], g_ref[...]) * (1.0 + mod_ref[0, 1:2, :]) + mod_ref[0, 0:1, :]
            st["h"] = h.astype(BF16)

        def gates(c0):
            c = col["g_a"] + c0
            gg_ref[rows, c0:c0 + 512] = _dot(st["h"], win_ref[:, c:c + 512]).astype(BF16)

        def k_moba():
            kb = _dot(st["h"], win_ref[:, col["b_k"]:col["b_k"] + B_W])
            for f in range(N_B_BLOCKS):
                kb_ref[f, rows, :] = kb[:, f * LANES:(f + 1) * LANES].astype(BF16)

        def k_win():
            ka_ref[rows, :] = _dot(st["h"], win_ref[:, col["a_k"]:col["a_k"] + A_KV_W]).astype(BF16)

        def q_win():
            qa = _dot_nt(wt_scr[0:A_Q_W, :], st["h"]) * scale
            for f in range(N_A_BLOCKS):
                for t in range(half // WINDOW):
                    qta_ref[f, t_win + t] = qa[f * LANES:(f + 1) * LANES,
                                               t * WINDOW:(t + 1) * WINDOW].astype(BF16)

        def q_moba():
            qb = _dot_nt(wt_scr[row_bq:row_bq + B_W, :], st["h"]) * scale
            for f in range(N_B_BLOCKS):
                qtb_ref[f, t_moba] = qb[f * LANES:(f + 1) * LANES, :].astype(BF16)

        def v_moba():
            vb = _dot_nt(wt_scr[row_bv:row_bv + B_W, :], st["h"])
            for f in range(N_B_BLOCKS):
                vtb_ref[f, t_moba] = vb[f * LANES:(f + 1) * LANES, :].astype(BF16)

        def v_win():
            va = _dot_nt(wt_scr[row_av:row_av + A_KV_W, :], st["h"])
            for t in range(half // WINDOW):
                vta_ref[t_win + t] = va[:, t * WINDOW:(t + 1) * WINDOW].astype(BF16)

        return ([pre] + [functools.partial(gates, c0) for c0 in range(0, 2 * d, 512)]
                + [k_moba, k_win, q_win, q_moba, v_moba, v_win])

    halves = [stages(r0) for r0 in range(0, tm, half)]
    n_steps = len(halves[0])
    for n in range(n_steps + len(halves) - 1):
        for lag, steps in enumerate(halves):
            if 0 <= n - lag < n_steps:
                steps[n - lag]()


def _inproj(x2, mod, g, w_in_bf16, seq, l):
    tokens, d = x2.shape
    tm = 512
    tiles_per_seq = seq // tm
    w_spec = _layer_block(w_in_bf16, l)
    out_shape = (
        jax.ShapeDtypeStruct((tokens, 2 * d), BF16),
        jax.ShapeDtypeStruct((N_B_BLOCKS, tokens, LANES), BF16),
        jax.ShapeDtypeStruct((tokens, LANES), BF16),
        jax.ShapeDtypeStruct((N_A_BLOCKS, tokens // WINDOW, LANES, WINDOW), BF16),
        jax.ShapeDtypeStruct((N_B_BLOCKS, tokens // MOBA_BLOCK, LANES, MOBA_BLOCK), BF16),
        jax.ShapeDtypeStruct((N_B_BLOCKS, tokens // MOBA_BLOCK, LANES, MOBA_BLOCK), BF16),
        jax.ShapeDtypeStruct((tokens // WINDOW, LANES, WINDOW), BF16),
    )
    out_specs = (
        pl.BlockSpec((tm, 2 * d), lambda i: (i, 0)),
        pl.BlockSpec((N_B_BLOCKS, tm, LANES), lambda i: (0, i, 0)),
        pl.BlockSpec((tm, LANES), lambda i: (i, 0)),
        pl.BlockSpec((N_A_BLOCKS, tm // WINDOW, LANES, WINDOW), lambda i: (0, i, 0, 0)),
        pl.BlockSpec((N_B_BLOCKS, tm // MOBA_BLOCK, LANES, MOBA_BLOCK), lambda i: (0, i, 0, 0)),
        pl.BlockSpec((N_B_BLOCKS, tm // MOBA_BLOCK, LANES, MOBA_BLOCK), lambda i: (0, i, 0, 0)),
        pl.BlockSpec((tm // WINDOW, LANES, WINDOW), lambda i: (i, 0, 0)),
    )
    return pl.pallas_call(
        _inproj_kernel,
        grid=(tokens // tm,),
        in_specs=[
            pl.BlockSpec((tm, d), lambda i: (i, 0)),
            pl.BlockSpec((None, 1, N_MOD, d), lambda i: (l, i // tiles_per_seq, 0, 0)),
            _layer_block(g, l),
            pl.BlockSpec(w_spec.block_shape, w_spec.index_map, pipeline_mode=pl.Buffered(1)),
        ],
        out_specs=out_specs,
        out_shape=out_shape,
        scratch_shapes=[pltpu.VMEM((WT_ROWS, d), BF16)],
        compiler_params=_params("arbitrary"),
        name="in_proj",
    )(x2, mod, g, w_in_bf16)


def _swa_kernel(qt_ref, k_ref, vt_ref, slope_ref, sink_ref, o_ref, bias_scr):
    n_blocks = qt_ref.shape[1]
    slope = slope_ref[0] * LOG2E
    sink = sink_ref[0] * LOG2E
    lo_rows = lax.broadcasted_iota(jnp.int32, (LANES, WINDOW), 0) < HEAD_DIM

    def bias(n_keys, offset):
        r = lax.broadcasted_iota(jnp.int32, (n_keys, 2 * WINDOW), 0)
        col = lax.broadcasted_iota(jnp.int32, (n_keys, 2 * WINDOW), 1)
        qpos = jnp.where(col >= WINDOW, col - WINDOW, col)
        dist = offset + qpos - r
        valid = (dist >= 0) & (dist < WINDOW)
        return jnp.where(valid, -slope * dist.astype(F32), NEG_INF)

    bias_scr[...] = bias(2 * WINDOW, WINDOW)
    bias_first = bias(WINDOW, 0)

    def scores(n):
        qt = qt_ref[0, n]
        zero = jnp.zeros_like(qt)
        qt2 = jnp.concatenate([jnp.where(lo_rows, qt, zero), jnp.where(lo_rows, zero, qt)], axis=1)
        if n == 0:
            return _dot(k_ref[0:WINDOW, :], qt2) + bias_first
        return _dot(k_ref[(n - 1) * WINDOW:(n + 1) * WINDOW, :], qt2) + bias_scr[...]

    def softmax(z):
        m = jnp.maximum(jnp.max(z, axis=0, keepdims=True), sink)
        p = jnp.exp2(z - m)
        denom = jnp.sum(p, axis=0, keepdims=True) + jnp.exp2(sink - m)
        return p.astype(BF16), denom

    def finish(n, p, denom):
        vt_win = vt_ref[0] if n == 0 else jnp.concatenate([vt_ref[n - 1], vt_ref[n]], axis=1)
        ot = _dot(vt_win, p) * (1.0 / denom)
        o_t = jnp.where(lo_rows, ot[:, :WINDOW], ot[:, WINDOW:])
        o_ref[0, n * WINDOW:(n + 1) * WINDOW, :] = o_t.T.astype(BF16)

    z = pd = None
    for s in range(n_blocks + 2):
        z_new = scores(s) if s < n_blocks else None
        pd_new = softmax(z) if 1 <= s <= n_blocks else None
        if s >= 2:
            finish(s - 2, *pd)
        z, pd = z_new, pd_new


def _swa(qta, ka, vta, slope_rows, sink_rows, batch, seq, l):
    tokens = ka.shape[0]
    nblk = seq // WINDOW
    return pl.pallas_call(
        _swa_kernel,
        grid=(batch, N_A_BLOCKS),
        in_specs=[
            pl.BlockSpec((1, nblk, LANES, WINDOW), lambda b, f: (f, b, 0, 0)),
            pl.BlockSpec((seq, LANES), lambda b, f: (b, 0)),
            pl.BlockSpec((nblk, LANES, WINDOW), lambda b, f: (b, 0, 0)),
            pl.BlockSpec((1, 1, 2 * WINDOW), lambda b, f: (f, 0, 0)),
            pl.BlockSpec((None, 1, 1, 2 * WINDOW), lambda b, f: (l, f, 0, 0)),
        ],
        out_specs=pl.BlockSpec((1, seq, LANES), lambda b, f: (f, b, 0)),
        out_shape=jax.ShapeDtypeStruct((N_A_BLOCKS, tokens, LANES), BF16),
        scratch_shapes=[pltpu.VMEM((2 * WINDOW, 2 * WINDOW), F32)],
        compiler_params=_params("parallel", "parallel"),
        name="swa_attn",
    )(qta, ka, vta, slope_rows, sink_rows)


def _moba_kernel(qt_ref, k_ref, vt_ref, slope_ref, o_ref, bias_scr, z_scr, p_scr, vt1_scr):
    nb = qt_ref.shape[1]
    blk = MOBA_BLOCK
    seq = nb * blk
    feat = lax.broadcasted_iota(jnp.int32, (LANES, blk), 0)
    head_rows = (feat < HEAD_DIM, feat >= HEAD_DIM)
    blk_row = lax.broadcasted_iota(jnp.int32, (nb, blk), 0)

    dist = (seq - blk
            + lax.broadcasted_iota(jnp.int32, (seq, blk), 1)
            - lax.broadcasted_iota(jnp.int32, (seq, blk), 0))
    for h in range(HEADS_PER_LANE_BLOCK):
        bias_scr[h] = jnp.where(dist >= 0, -(slope_ref[0, h] * LOG2E) * dist.astype(F32), NEG_INF)

    tok = lax.broadcasted_iota(jnp.int32, (nb, seq), 1)
    blk_start = lax.broadcasted_iota(jnp.int32, (nb, seq), 0) * blk
    avg = jnp.where((tok >= blk_start) & (tok < blk_start + blk), 1.0 / blk, 0.0)
    k_mean = _dot(avg.astype(BF16), k_ref[0])
    km_hi = k_mean.astype(BF16)
    km_lo = (k_mean - km_hi.astype(F32)).astype(BF16)

    for j in range(nb):
        vt = vt_ref[0, j]
        for h in range(HEADS_PER_LANE_BLOCK):
            vt1_scr[h, j] = jnp.where(head_rows[h], vt, jnp.ones_like(vt))

    tasks = [(i, h) for i in range(nb) for h in range(HEADS_PER_LANE_BLOCK)]
    state = [dict() for _ in tasks]

    def scores_begin(t):
        i, h = tasks[t]
        qt = qt_ref[0, i]
        qt_h = jnp.where(head_rows[h], qt, jnp.zeros_like(qt))
        st = state[t]
        st["qt"] = qt_h
        st["cm"] = []
        if i > MOBA_TOPK:
            gate = _dot(km_hi, qt_h) + _dot(km_lo, qt_h)
            rank = jnp.zeros((nb, blk), jnp.int32)
            for jp in range(i):
                g_jp = gate[jp:jp + 1, :]
                beats = (g_jp > gate) | ((g_jp == gate) & (jp < blk_row))
                rank = rank + jnp.where(beats, 1, 0)
            st["unsel"] = jnp.where(rank < MOBA_TOPK, 0.0, NEG_INF)

    def scores_tile(t, j):
        i, h = tasks[t]
        st = state[t]
        row0 = seq - blk * (i + 1) + blk * j
        z = _dot(k_ref[0, j * blk:(j + 1) * blk, :], st["qt"]) + bias_scr[h, row0:row0 + blk, :]
        z_scr[t % 2, j] = z
        cm = jnp.max(z, axis=0, keepdims=True)
        if "unsel" in st and j < i:
            cm = cm + st["unsel"][j:j + 1, :]
        st["cm"].append(cm)

    def scores_end(t):
        i, h = tasks[t]
        st = state[t]
        m = functools.reduce(jnp.maximum, st["cm"])
        st["shift"] = [m - st["unsel"][j:j + 1, :] if ("unsel" in st and j < i) else m
                       for j in range(i + 1)]

    def probs_tile(t, j):
        p = jnp.exp2(z_scr[t % 2, j] - state[t]["shift"][j])
        p_scr[t % 2, j] = p.astype(BF16)

    def values_tile(t, j):
        i, h = tasks[t]
        st = state[t]
        part = _dot(vt1_scr[h, j], p_scr[t % 2, j])
        st["acc"] = part if j == 0 else st["acc"] + part

    def values_end(t):
        i, h = tasks[t]
        if h != HEADS_PER_LANE_BLOCK - 1:
            return
        acc0, acc1 = state[t - 1]["acc"], state[t]["acc"]
        denom0 = acc0[HEAD_DIM:HEAD_DIM + 1, :]
        denom1 = acc1[0:1, :]
        o_t = jnp.where(head_rows[0], acc0 * (1.0 / denom0), acc1 * (1.0 / denom1))
        o_ref[0, i * blk:(i + 1) * blk, :] = o_t.T.astype(BF16)
        state[t - 1].clear()
        state[t].clear()

    n_tasks = len(tasks)
    for step in range(n_tasks + 2):
        t_s, t_p, t_v = step, step - 1, step - 2
        live = [t for t in (t_s, t_p, t_v) if 0 <= t < n_tasks]
        if 0 <= t_s < n_tasks:
            scores_begin(t_s)
        for j in range(max(tasks[t][0] + 1 for t in live)):
            if 0 <= t_s < n_tasks and j <= tasks[t_s][0]:
                scores_tile(t_s, j)
            if 0 <= t_p < n_tasks and j <= tasks[t_p][0]:
                probs_tile(t_p, j)
            if 0 <= t_v < n_tasks and j <= tasks[t_v][0]:
                values_tile(t_v, j)
        if 0 <= t_s < n_tasks:
            scores_end(t_s)
        if 0 <= t_v < n_tasks:
            values_end(t_v)


def _moba(qtb, kb, vtb, slope_rows, batch, seq):
    tokens = kb.shape[1]
    nb = seq // MOBA_BLOCK
    return pl.pallas_call(
        _moba_kernel,
        grid=(batch, N_B_BLOCKS),
        in_specs=[
            pl.BlockSpec((1, nb, LANES, MOBA_BLOCK), lambda b, f: (f, b, 0, 0)),
            pl.BlockSpec((1, seq, LANES), lambda b, f: (f, b, 0)),
            pl.BlockSpec((1, nb, LANES, MOBA_BLOCK), lambda b, f: (f, b, 0, 0)),
            pl.BlockSpec((1, HEADS_PER_LANE_BLOCK, 1, MOBA_BLOCK), lambda b, f: (f, 0, 0, 0)),
        ],
        out_specs=pl.BlockSpec((1, seq, LANES), lambda b, f: (f, b, 0)),
        out_shape=jax.ShapeDtypeStruct((N_B_BLOCKS, tokens, LANES), BF16),
        scratch_shapes=[
            pltpu.VMEM((HEADS_PER_LANE_BLOCK, seq, MOBA_BLOCK), F32),
            pltpu.VMEM((2, nb, MOBA_BLOCK, MOBA_BLOCK), F32),
            pltpu.VMEM((2, nb, MOBA_BLOCK, MOBA_BLOCK), BF16),
            pltpu.VMEM((HEADS_PER_LANE_BLOCK, nb, LANES, MOBA_BLOCK), BF16),
        ],
        compiler_params=_params("parallel", "parallel"),
        name="moba_attn",
    )(qtb, kb, vtb, slope_rows)


FF_CHUNK = 512
OUT_CHUNK = 512


def _post_kernel(oa_ref, ob_ref, gg_ref, x_ref, mod_ref, gmix_ref, gpre_ref, gpost_ref,
                 woa_ref, wob_ref, wout_ref, wgu_ref, wd_ref, o_ref, act_scr, x1_scr):
    tm, d = x_ref.shape
    d_ff = wd_ref.shape[0]
    half = tm // 2
    ff_chunks = [(c0, min(c0 + FF_CHUNK, d_ff)) for c0 in range(0, d_ff, FF_CHUNK)]
    out_chunks = [(n0, n0 + OUT_CHUNK) for n0 in range(0, d, OUT_CHUNK)]

    def stages(r0):
        rows = slice(r0, r0 + half)
        st = {"y": []}

        def merge():
            oa = jnp.concatenate([oa_ref[f, rows, :] for f in range(N_A_BLOCKS)], axis=1)
            ob = jnp.concatenate([ob_ref[f, rows, :] for f in range(N_B_BLOCKS)], axis=1)
            gg = gg_ref[rows, :].astype(F32)
            st["merged"] = (jax.nn.sigmoid(gg[:, :d]) * _dot(oa, woa_ref[...])
                            + jax.nn.sigmoid(gg[:, d:]) * _dot(ob, wob_ref[...])).astype(BF16)

        def mix_residual():
            y = _dot(st.pop("merged"), wout_ref[...])
            x1_scr[rows, :] = x_ref[rows, :] + mod_ref[0, 2:3, :] * _rms(y, gmix_ref[...])

        def ffn_in():
            st["h"] = (_rms(x1_scr[rows, :], gpre_ref[...]) * (1.0 + mod_ref[0, 4:5, :])
                       + mod_ref[0, 3:4, :]).astype(BF16)

        def gate_up(c0, c1):
            gate = _dot(st["h"], wgu_ref[:, c0:c1])
            up = _dot(st["h"], wgu_ref[:, d_ff + c0:d_ff + c1])
            act_scr[rows, c0:c1] = (gate * jax.nn.sigmoid(gate) * up).astype(BF16)

        def down(n0, n1):
            st["y"].append(_dot(act_scr[rows, :], wd_ref[:, n0:n1]))

        def ffn_residual():
            y = jnp.concatenate(st["y"], axis=1)
            o_ref[rows, :] = x1_scr[rows, :] + mod_ref[0, 5:6, :] * _rms(y, gpost_ref[...])

        return ([merge, mix_residual, ffn_in]
                + [functools.partial(gate_up, c0, c1) for c0, c1 in ff_chunks]
                + [functools.partial(down, n0, n1) for n0, n1 in out_chunks]
                + [ffn_residual])

    first, second = stages(0), stages(half)
    lag = 3
    for n in range(len(first) + lag):
        if n < len(first):
            first[n]()
        if n >= lag:
            second[n - lag]()


def _post(oa, ob, gg, x2, mod, gmix, gpre, gpost, woa, wob, wout, wgu, wd, seq, l):
    tokens, d = x2.shape
    d_ff = wd.shape[1]
    tm = 512
    tiles_per_seq = seq // tm

    def resident(arr):
        spec = _layer_block(arr, l)
        return pl.BlockSpec(spec.block_shape, spec.index_map, pipeline_mode=pl.Buffered(1))

    return pl.pallas_call(
        _post_kernel,
        grid=(tokens // tm,),
        in_specs=[
            pl.BlockSpec((N_A_BLOCKS, tm, LANES), lambda i: (0, i, 0)),
            pl.BlockSpec((N_B_BLOCKS, tm, LANES), lambda i: (0, i, 0)),
            pl.BlockSpec((tm, 2 * d), lambda i: (i, 0)),
            pl.BlockSpec((tm, d), lambda i: (i, 0)),
            pl.BlockSpec((None, 1, N_MOD, d), lambda i: (l, i // tiles_per_seq, 0, 0)),
            _layer_block(gmix, l),
            _layer_block(gpre, l),
            _layer_block(gpost, l),
            resident(woa),
            resident(wob),
            resident(wout),
            resident(wgu),
            resident(wd),
        ],
        out_specs=pl.BlockSpec((tm, d), lambda i: (i, 0)),
        out_shape=jax.ShapeDtypeStruct((tokens, d), F32),
        scratch_shapes=[pltpu.VMEM((tm, d_ff), BF16), pltpu.VMEM((tm, d), F32)],
        compiler_params=_params("parallel"),
        name="mix_ffn",
    )(oa, ob, gg, x2, mod, gmix, gpre, gpost, woa, wob, wout, wgu, wd)


def _alibi_slopes():
    i = np.arange(1, N_ALIBI + 1, dtype=np.float32)
    s = (2.0 ** (-8.0 * i / N_ALIBI)).astype(np.float32)
    return s[:A_Q_HEADS], s[A_Q_HEADS:]


def kernel(x, c, ada_w, ada_b, norm_pre_mix, norm_post_mix, w_in, attn_sinks, w_o_a, w_o_b, w_out,
           norm_pre_ffn, norm_post_ffn, w_gate_up, w_down):
    batch, seq, d = x.shape
    depth = ada_w.shape[0]
    assert seq % 512 == 0 and d % LANES == 0

    slopes_a, slopes_b = _alibi_slopes()
    a_order = np.asarray(A_HEAD_ORDER).reshape(N_A_BLOCKS, HEADS_PER_LANE_BLOCK)
    slope_a_rows = jnp.asarray(np.repeat(slopes_a[a_order], WINDOW, axis=1)[:, None, :])
    slope_b_rows = jnp.asarray(np.broadcast_to(
        slopes_b.reshape(N_B_BLOCKS, HEADS_PER_LANE_BLOCK, 1, 1),
        (N_B_BLOCKS, HEADS_PER_LANE_BLOCK, 1, MOBA_BLOCK)))

    sink_rows = jnp.repeat(attn_sinks.reshape(depth, A_KV_HEADS, A_GROUP).swapaxes(1, 2),
                           WINDOW, axis=2).reshape(depth, N_A_BLOCKS, 1, 2 * WINDOW)

    w_in_bf16 = w_in.astype(BF16)
    woa =(w_o_a.reshape(depth, A_KV_HEADS, A_GROUP, HEAD_DIM, d).swapaxes(1, 2)
           .reshape(depth, A_Q_W, d).astype(BF16))
    wob, wout = w_o_b.astype(BF16), w_out.astype(BF16)
    wgu, wd = w_gate_up.astype(BF16), w_down.astype(BF16)
    g_pre_mix, g_post_mix, g_pre_ffn, g_post_ffn = (
        g.reshape(depth, 1, d) for g in (norm_pre_mix, norm_post_mix, norm_pre_ffn, norm_post_ffn))

    mod = _ada(c, ada_w, ada_b).reshape(depth, batch, N_MOD, d)
    x2 = x.reshape(batch * seq, d)
    for l in range(depth):
        gg, kb, ka, qta, qtb, vtb, vta = _inproj(x2, mod, g_pre_mix, w_in_bf16, seq, l)
        oa = _swa(qta, ka, vta, slope_a_rows, sink_rows, batch, seq, l)
        ob = _moba(qtb, kb, vtb, slope_b_rows, batch, seq)
        x2 = _post(oa, ob, gg, x2, mod, g_post_mix, g_pre_ffn, g_post_ffn, woa, wob, wout, wgu, wd, seq, l)
    return x2.reshape(batch, seq, d)
```
